```python
import jax, jax.numpy as jnp
from jax import lax
import numpy as np

D_MODEL = 2048
BATCH = 4
SEQ = 4096
DEPTH = 1

HEAD_DIM = 128
N_HEADS_MIX = D_MODEL // HEAD_DIM
N_HEADS_ATTN = 3 * N_HEADS_MIX // 4
N_HEADS_SGU = N_HEADS_MIX - N_HEADS_ATTN
D_ATTN = N_HEADS_ATTN * HEAD_DIM
D_SGU = N_HEADS_SGU * HEAD_DIM
D_IN = 3 * D_ATTN + 2 * D_SGU
DILATION_PATTERNS = ((128, 1), (512, 4), (2048, 16))
ROPE_THETA = 10000.0
SGU_CHUNK = 128
PEER_HEADS = 8
PEER_NKEYS = 128
PEER_EXPERTS = PEER_NKEYS * PEER_NKEYS
PEER_DKEY = 256
PEER_TOPK = 16
PEER_TOKEN_BLOCK = 128
N_MOD = 6
RMS_EPS = 1e-6
LN_EPS = 1e-5

kernel_name = "hybrid_dilated_attn_sgu_peer_block"


def rms_norm(t, g):
    t32 = t.astype(jnp.float32)
    t32 = t32 * lax.rsqrt(jnp.mean(t32 * t32, axis=-1, keepdims=True) + RMS_EPS)
    return t32.astype(t.dtype) * g


def rope(t, positions):
    half = t.shape[-1] // 2
    inv = ROPE_THETA ** (-jnp.arange(half, dtype=jnp.float32) / half)
    ang = positions.astype(jnp.float32)[..., None] * inv
    cos = jnp.cos(ang)[:, :, None, :]
    sin = jnp.sin(ang)[:, :, None, :]
    t32 = t.astype(jnp.float32)
    t1, t2 = t32[..., :half], t32[..., half:]
    return jnp.concatenate([t1 * cos - t2 * sin, t2 * cos + t1 * sin], axis=-1).astype(t.dtype)


def dilated_window_attention(q, k, v, window, dilation):
    B, S, H, E = q.shape
    steps = window // dilation
    L = S // dilation
    nb = -(-L // steps)
    Lp = nb * steps

    def to_classes(t):
        t = t.reshape(B, L, dilation, H, E).transpose(0, 2, 1, 3, 4)
        return jnp.pad(t, ((0, 0), (0, 0), (0, Lp - L), (0, 0), (0, 0)))

    def with_prev(t):
        tb = t.reshape(B, dilation, nb, steps, H, E)
        prev = jnp.pad(tb, ((0, 0), (0, 0), (1, 0), (0, 0), (0, 0), (0, 0)))[:, :, :-1]
        return jnp.concatenate([prev, tb], axis=3)

    qb = to_classes(q).reshape(B, dilation, nb, steps, H, E)
    kb = with_prev(to_classes(k))
    vb = with_prev(to_classes(v))
    s = jnp.einsum('bdnqhe,bdnkhe->bdnhqk', qb, kb).astype(jnp.float32)
    qi = jnp.arange(steps)[:, None]
    km = jnp.arange(2 * steps)[None, :]
    band = (km >= qi) & (km <= qi + steps)
    valid = band[None] & ((jnp.arange(nb)[:, None, None] > 0) | (km[None] >= steps))
    s = jnp.where(valid[None, None, :, None], s, -jnp.inf)
    lse = jax.nn.logsumexp(s, axis=-1)
    p = jnp.exp(s - lse[..., None])
    o = jnp.einsum('bdnhqk,bdnkhe->bdnqhe', p.astype(v.dtype), vb)
    o = o.reshape(B, dilation, Lp, H, E)[:, :, :L].transpose(0, 2, 1, 3, 4).reshape(B, S, H, E)
    lse = lse.transpose(0, 1, 2, 4, 3).reshape(B, dilation, Lp, H)[:, :, :L]
    lse = lse.transpose(0, 2, 1, 3).reshape(B, S, H)
    return o, lse


def causal_chunk_sgu(u, v, w_s, b_s, ln_g, ln_b):
    B, S, Hs, E = u.shape
    n = S // SGU_CHUNK
    v32 = v.astype(jnp.float32)
    mu = jnp.mean(v32, axis=-1, keepdims=True)
    var = jnp.mean(jnp.square(v32 - mu), axis=-1, keepdims=True)
    vn = ((v32 - mu) * lax.rsqrt(var + LN_EPS)).astype(v.dtype) * ln_g + ln_b
    vc = vn.reshape(B, n, SGU_CHUNK, Hs, E)
    causal = jnp.tril(jnp.ones((SGU_CHUNK, SGU_CHUNK), dtype=bool))
    ws = jnp.where(causal[None], w_s, jnp.zeros((), w_s.dtype))
    mixed = jnp.einsum('hij,bnjhe->bnihe', ws, vc) + b_s.T[None, None, :, :, None]
    return u * mixed.reshape(B, S, Hs, E)


def peer_ffn(h, w_q, sub_keys, u_emb, v_emb):
    B, S, D = h.shape
    T = B * S
    K = PEER_TOPK
    x = h.reshape(T, D)
    q = (x @ w_q).reshape(T, PEER_HEADS, 2, PEER_DKEY // 2)
    s = jnp.einsum('thpc,hpkc->thpk', q, sub_keys).astype(jnp.float32)
    s_top, i_top = lax.top_k(s, K)
    cand_s = s_top[:, :, 0, :, None] + s_top[:, :, 1, None, :]
    cand_i = i_top[:, :, 0, :, None] * PEER_NKEYS + i_top[:, :, 1, None, :]
    best_s, pos = lax.top_k(cand_s.reshape(T, PEER_HEADS, K * K), K)
    idx = jnp.take_along_axis(cand_i.reshape(T, PEER_HEADS, K * K), pos, axis=-1)
    g = jax.nn.softmax(best_s, axis=-1).astype(h.dtype)
    nblk = T // PEER_TOKEN_BLOCK
    HK = PEER_HEADS * K

    def expert_block(args):
        xb, ib, gb = args
        ue = jnp.take(u_emb, ib, axis=0)
        a = jax.nn.gelu(jnp.einsum('td,tnd->tn', xb, ue), approximate=False)
        ve = jnp.take(v_emb, ib, axis=0)
        return jnp.einsum('tn,tnd->td', gb * a, ve)

    out = lax.map(expert_block, (x.reshape(nblk, PEER_TOKEN_BLOCK, D),
                                 idx.reshape(nblk, PEER_TOKEN_BLOCK, HK),
                                 g.reshape(nblk, PEER_TOKEN_BLOCK, HK)))
    return out.reshape(B, S, D)


def setup_inputs(seed: int = 0) -> dict:
    key = jax.random.key(seed)
    ks = jax.random.split(key, 24)
    f32 = jnp.float32
    D = D_MODEL
    nrm = lambda k, shape, scale: jax.random.normal(k, shape, f32) * scale
    x = jax.random.normal(ks[0], (BATCH, SEQ, D), f32)
    c = jax.random.normal(ks[1], (BATCH, D), f32)
    offset = jax.random.randint(ks[2], (BATCH, 1), 0, 1024, dtype=jnp.int32)
    positions = offset + jnp.arange(SEQ, dtype=jnp.int32)[None, :]
    return {
        "x": x,
        "c": c,
        "positions": positions,
        "w_ada": nrm(ks[3], (DEPTH, D, N_MOD * D), 0.5 * D ** -0.5),
        "b_ada": nrm(ks[4], (DEPTH, N_MOD * D), 0.02),
        "g_norm1": 1.0 + nrm(ks[5], (DEPTH, D), 0.05),
        "w_in": nrm(ks[6], (DEPTH, D, D_IN), D ** -0.5),
        "g_attn_out": 1.0 + nrm(ks[7], (DEPTH, D_ATTN), 0.05),
        "g_sgu_out": 1.0 + nrm(ks[8], (DEPTH, D_SGU), 0.05),
        "sgu_w": nrm(ks[9], (DEPTH, N_HEADS_SGU, SGU_CHUNK, SGU_CHUNK), SGU_CHUNK ** -0.5),
        "sgu_b": 1.0 + nrm(ks[10], (DEPTH, N_HEADS_SGU, SGU_CHUNK), 0.1),
        "sgu_ln_g": 1.0 + nrm(ks[11], (DEPTH, N_HEADS_SGU, HEAD_DIM), 0.05),
        "sgu_ln_b": nrm(ks[12], (DEPTH, N_HEADS_SGU, HEAD_DIM), 0.02),
        "w_out": nrm(ks[13], (DEPTH, D, D), D ** -0.5),
        "g_norm2": 1.0 + nrm(ks[14], (DEPTH, D), 0.05),
        "peer_w_q": nrm(ks[15], (DEPTH, D, PEER_HEADS * PEER_DKEY), D ** -0.5),
        "peer_sub_keys": nrm(ks[16], (DEPTH, PEER_HEADS, 2, PEER_NKEYS, PEER_DKEY // 2), (PEER_DKEY // 2) ** -0.5),
        "peer_u": nrm(ks[17], (DEPTH, PEER_EXPERTS, D), D ** -0.5),
        "peer_v": nrm(ks[18], (DEPTH, PEER_EXPERTS, D), 0.5),
        "g_final": 1.0 + nrm(ks[19], (D,), 0.05),
    }


def reference(x, c, positions, w_ada, b_ada, g_norm1, w_in, g_attn_out, g_sgu_out,
              sgu_w, sgu_b, sgu_ln_g, sgu_ln_b, w_out, g_norm2, peer_w_q, peer_sub_keys,
              peer_u, peer_v, g_final):
    B, S, D = x.shape
    E = HEAD_DIM
    h = x
    cond = jax.nn.silu(c)
    splits = np.cumsum([D_ATTN, D_ATTN, D_ATTN, D_SGU]).tolist()
    for l in range(DEPTH):
        mod = (cond @ w_ada[l] + b_ada[l]).reshape(B, N_MOD, D)
        sh1, sc1, gt1, sh2, sc2, gt2 = [mod[:, i, None, :] for i in range(N_MOD)]

        hn = rms_norm(h, g_norm1[l]) * (1.0 + sc1) + sh1
        proj = hn @ w_in[l]
        qa, ka, va, us, vs = jnp.split(proj, splits, axis=-1)
        qa = rope(qa.reshape(B, S, N_HEADS_ATTN, E), positions) * (E ** -0.5)
        ka = rope(ka.reshape(B, S, N_HEADS_ATTN, E), positions)
        va = va.reshape(B, S, N_HEADS_ATTN, E)
        outs, lses = [], []
        for window, dilation in DILATION_PATTERNS:
            o_i, lse_i = dilated_window_attention(qa, ka, va, window, dilation)
            outs.append(o_i)
            lses.append(lse_i)
        wts = jax.nn.softmax(jnp.stack(lses, axis=0), axis=0)
        attn = sum(wts[i][..., None] * outs[i].astype(jnp.float32) for i in range(len(outs)))
        attn = attn.astype(x.dtype).reshape(B, S, D_ATTN)

        us = jax.nn.gelu(us, approximate=False).reshape(B, S, N_HEADS_SGU, E)
        vs = jax.nn.gelu(vs, approximate=False).reshape(B, S, N_HEADS_SGU, E)
        sgu = causal_chunk_sgu(us, vs, sgu_w[l], sgu_b[l], sgu_ln_g[l], sgu_ln_b[l]).reshape(B, S, D_SGU)

        mixed = jnp.concatenate([rms_norm(attn, g_attn_out[l]), rms_norm(sgu, g_sgu_out[l])], axis=-1)
        h = h + gt1 * (mixed @ w_out[l])

        hn2 = rms_norm(h, g_norm2[l]) * (1.0 + sc2) + sh2
        h = h + gt2 * peer_ffn(hn2, peer_w_q[l], peer_sub_keys[l], peer_u[l], peer_v[l])
    return rms_norm(h, g_final)
```

```python
import functools
import math

import jax
import jax.numpy as jnp
from jax import lax
from jax.experimental import pallas as pl
from jax.experimental.pallas import tpu as pltpu

F32 = jnp.float32
BF16 = jnp.bfloat16

HEAD_DIM = 128
N_HEADS_ATTN = 12
N_HEADS_SGU = 4
D_ATTN = N_HEADS_ATTN * HEAD_DIM
D_SGU = N_HEADS_SGU * HEAD_DIM
D_QKV = 3 * D_ATTN
DILATION_PATTERNS = ((128, 1), (512, 4), (2048, 16))
ATTN_STEPS = 128
ROPE_THETA = 10000.0
SGU_CHUNK = 128
PEER_HEADS = 8
PEER_NKEYS = 128
PEER_TOPK = 16
N_MOD = 6
RMS_EPS = 1e-6
LN_EPS = 1e-5

LANES = 128
SUBLANES = 8
VMEM_LIMIT = 56 * 1024 * 1024
NEG_INF = float("-inf")


def _params(semantics):
    return pltpu.CompilerParams(dimension_semantics=semantics, vmem_limit_bytes=VMEM_LIMIT)


def _gelu(x):
    return 0.5 * x * (1.0 + lax.erf(x * (1.0 / math.sqrt(2.0))))


def _rms(x, eps):
    return x * lax.rsqrt(jnp.mean(x * x, axis=-1, keepdims=True) + eps)


def _adaln_kernel(c_ref, w_ref, b_ref, o_ref):
    c = c_ref[...]
    cond = c * jax.nn.sigmoid(c)
    o_ref[...] = jnp.dot(cond.astype(BF16), w_ref[...].astype(BF16),
                         preferred_element_type=F32) + b_ref[...]


def _adaln(c, w_ada, b_ada):
    B, D = c.shape
    N = w_ada.shape[1]
    tn = 1024
    rows = SUBLANES
    c_pad = jnp.pad(c, ((0, rows - B), (0, 0)))
    out = pl.pallas_call(
        _adaln_kernel,
        out_shape=jax.ShapeDtypeStruct((rows, N), F32),
        grid=(N // tn,),
        in_specs=[pl.BlockSpec((rows, D), lambda j: (0, 0)),
                  pl.BlockSpec((D, tn), lambda j: (0, j)),
                  pl.BlockSpec((1, tn), lambda j: (0, j))],
        out_specs=pl.BlockSpec((rows, tn), lambda j: (0, j)),
        compiler_params=_params(("arbitrary",)),
        name="adaln",
    )(c_pad, w_ada, b_ada.reshape(1, N))
    return out[:B]


INPROJ_TM = 1024
INPROJ_TN = 512
_QK_TILES = 2 * D_ATTN // INPROJ_TN
_Q_TILES = D_ATTN // INPROJ_TN
_QKV_TILES = D_QKV // INPROJ_TN


def _inproj_kernel(x_ref, pos_ref, inv_ref, sc_ref, sh_ref, g_ref, w_ref,
                   qkv_ref, uv_ref, hn_ref, cos_ref, sin_ref):
    j = pl.program_id(2)

    @pl.when(j == 0)
    def _():
        x = x_ref[0]
        hn = _rms(x, RMS_EPS) * g_ref[...]
        hn = hn * (1.0 + sc_ref[0]) + sh_ref[0]
        hn_ref[...] = hn.astype(BF16)
        ang = pos_ref[0] * inv_ref[...]
        lane = lax.broadcasted_iota(jnp.int32, ang.shape, 1)
        cos_ref[...] = jnp.cos(ang)
        sin_ref[...] = jnp.where(lane < HEAD_DIM // 2, -1.0, 1.0) * jnp.sin(ang)

    acc = jnp.dot(hn_ref[...], w_ref[...], preferred_element_type=F32)

    @pl.when(j < _QK_TILES)
    def _():
        scale = jnp.where(j < _Q_TILES, HEAD_DIM ** -0.5, 1.0).astype(F32)
        cos = cos_ref[...]
        sin = sin_ref[...]
        for h in range(INPROJ_TN // HEAD_DIM):
            t = acc[:, h * HEAD_DIM:(h + 1) * HEAD_DIM]
            r = t * cos + pltpu.roll(t, HEAD_DIM // 2, 1) * sin
            qkv_ref[0, :, h * HEAD_DIM:(h + 1) * HEAD_DIM] = (r * scale).astype(BF16)

    @pl.when(jnp.logical_and(j >= _QK_TILES, j < _QKV_TILES))
    def _():
        qkv_ref[0] = acc.astype(BF16)

    @pl.when(j >= _QKV_TILES)
    def _():
        uv_ref[0] = _gelu(acc)


def _inproj(x, posf, inv2, sc1, sh1, g1, w_in):
    B, S, D = x.shape
    N = w_in.shape[1]
    tm, tn = INPROJ_TM, INPROJ_TN
    last_qkv = _QKV_TILES - 1
    return pl.pallas_call(
        _inproj_kernel,
        out_shape=(jax.ShapeDtypeStruct((B, S, D_QKV), BF16),
                   jax.ShapeDtypeStruct((B, S, 2 * D_SGU), F32)),
        grid=(B, S // tm, N // tn),
        in_specs=[pl.BlockSpec((1, tm, D), lambda b, i, j: (b, i, 0)),
                  pl.BlockSpec((1, tm, 1), lambda b, i, j: (b, i, 0)),
                  pl.BlockSpec((1, LANES), lambda b, i, j: (0, 0)),
                  pl.BlockSpec((1, 1, D), lambda b, i, j: (b, 0, 0)),
                  pl.BlockSpec((1, 1, D), lambda b, i, j: (b, 0, 0)),
                  pl.BlockSpec((1, D), lambda b, i, j: (0, 0)),
                  pl.BlockSpec((D, tn), lambda b, i, j: (0, j))],
        out_specs=(pl.BlockSpec((1, tm, tn), lambda b, i, j: (b, i, jnp.minimum(j, last_qkv))),
                   pl.BlockSpec((1, tm, tn), lambda b, i, j: (b, i, jnp.maximum(j - _QKV_TILES, 0)))),
        scratch_shapes=[pltpu.VMEM((tm, D), BF16),
                        pltpu.VMEM((tm, LANES), F32),
                        pltpu.VMEM((tm, LANES), F32)],
        compiler_params=_params(("arbitrary", "arbitrary", "arbitrary")),
        name="inproj",
    )(x, posf, inv2, sc1, sh1, g1, w_in)


def _attn_kernel(q_ref, k_ref, v_ref, o_ref, qf, kf, vf, qd, kd, vd, acc, m_s, l_s):
    S = q_ref.shape[1]
    steps = ATTN_STEPS
    n_blocks = S // steps
    qf[...] = q_ref[0].astype(F32)
    kf[...] = k_ref[0].astype(F32)
    vf[...] = v_ref[0].astype(F32)
    kd[0:steps, :] = jnp.zeros((steps, HEAD_DIM), BF16)
    vd[0:steps, :] = jnp.zeros((steps, HEAD_DIM), BF16)

    qi = lax.broadcasted_iota(jnp.int32, (steps, 2 * steps), 0)
    km = lax.broadcasted_iota(jnp.int32, (steps, 2 * steps), 1)
    band = jnp.logical_and(km >= qi, km <= qi + steps)

    for pi, (window, d) in enumerate(DILATION_PATTERNS):
        assert window // d == steps
        L = S // d
        nb = L // steps
        for r in range(d):
            qd[r * L:(r + 1) * L, :] = qf[pl.ds(r, L, stride=d), :].astype(BF16)
            kd[steps + r * L:steps + (r + 1) * L, :] = kf[pl.ds(r, L, stride=d), :].astype(BF16)
            vd[steps + r * L:steps + (r + 1) * L, :] = vf[pl.ds(r, L, stride=d), :].astype(BF16)

        def body(g, carry, pi=pi, d=d, nb=nb):
            n = g % nb
            r = g // nb
            row0 = pl.multiple_of(g * steps, steps)
            qb = qd[pl.ds(row0, steps), :]
            kc = kd[pl.ds(row0, 2 * steps), :]
            vc = vd[pl.ds(row0, 2 * steps), :]
            s = lax.dot_general(qb, kc, (((1,), (1,)), ((), ())), preferred_element_type=F32)
            first_key = jnp.where(n > 0, 0, steps)
            s = jnp.where(jnp.logical_and(band, km >= first_key), s, NEG_INF)
            mb = jnp.max(s, axis=-1, keepdims=True)
            p = jnp.exp(s - mb)
            lb = jnp.sum(p, axis=-1, keepdims=True)
            ob = jnp.dot(p.astype(BF16), vc, preferred_element_type=F32)
            rows = pl.ds(n * (steps * d) + r, steps, stride=d)
            mb_b = jnp.broadcast_to(mb, (steps, HEAD_DIM))
            lb_b = jnp.broadcast_to(lb, (steps, HEAD_DIM))
            if pi == 0:
                acc[rows, :] = ob
                m_s[rows, :] = mb_b
                l_s[rows, :] = lb_b
            else:
                m_old = m_s[rows, :]
                m_new = jnp.maximum(m_old, mb_b)
                a_old = jnp.exp(m_old - m_new)
                a_blk = jnp.exp(mb_b - m_new)
                acc[rows, :] = acc[rows, :] * a_old + ob * a_blk
                l_s[rows, :] = l_s[rows, :] * a_old + lb_b * a_blk
                m_s[rows, :] = m_new
            return carry

        lax.fori_loop(0, n_blocks, body, 0)

    o_ref[0] = acc[...] / l_s[...]


def _attention(qkv):
    B, S, _ = qkv.shape
    H = N_HEADS_ATTN
    blk = lambda off: pl.BlockSpec((1, S, HEAD_DIM), lambda b, h: (b, 0, off + h))
    return pl.pallas_call(
        _attn_kernel,
        out_shape=jax.ShapeDtypeStruct((B, S, D_ATTN), F32),
        grid=(B, H),
        in_specs=[blk(0), blk(H), blk(2 * H)],
        out_specs=pl.BlockSpec((1, S, HEAD_DIM), lambda b, h: (b, 0, h)),
        scratch_shapes=[pltpu.VMEM((S, HEAD_DIM), F32),
                        pltpu.VMEM((S, HEAD_DIM), F32),
                        pltpu.VMEM((S, HEAD_DIM), F32),
                        pltpu.VMEM((S, HEAD_DIM), BF16),
                        pltpu.VMEM((S + ATTN_STEPS, HEAD_DIM), BF16),
                        pltpu.VMEM((S + ATTN_STEPS, HEAD_DIM), BF16),
                        pltpu.VMEM((S, HEAD_DIM), F32),
                        pltpu.VMEM((S, HEAD_DIM), F32),
                        pltpu.VMEM((S, HEAD_DIM), F32)],
        compiler_params=_params(("arbitrary", "arbitrary")),
        name="attn",
    )(qkv, qkv, qkv)


SGU_TS = 512


def _sgu_kernel(u_ref, v_ref, w_ref, b_ref, lg_ref, lb_ref, o_ref):
    C = SGU_CHUNK
    n_chunks = u_ref.shape[1] // C
    row = lax.broadcasted_iota(jnp.int32, (C, C), 0)
    col = lax.broadcasted_iota(jnp.int32, (C, C), 1)
    for h in range(N_HEADS_SGU):
        cols = slice(h * HEAD_DIM, (h + 1) * HEAD_DIM)
        v = v_ref[0, :, cols]
        mu = jnp.mean(v, axis=-1, keepdims=True)
        vc = v - mu
        var = jnp.mean(vc * vc, axis=-1, keepdims=True)
        vn = vc * lax.rsqrt(var + LN_EPS) * lg_ref[h:h + 1, :] + lb_ref[h:h + 1, :]
        vn = vn.astype(BF16)
        ws = jnp.where(row >= col, w_ref[h], 0.0).astype(BF16)
        rhs = jnp.concatenate([vn[n * C:(n + 1) * C, :] for n in range(n_chunks)], axis=1)
        mixed = jnp.dot(ws, rhs, preferred_element_type=F32)
        for n in range(n_chunks):
            gate = mixed[:, n * HEAD_DIM:(n + 1) * HEAD_DIM] + b_ref[h]
            o_ref[0, n * C:(n + 1) * C, cols] = u_ref[0, n * C:(n + 1) * C, cols] * gate


def _sgu(uv, sgu_w, sgu_b_lanes, ln_g, ln_b):
    B, S, _ = uv.shape
    ts = SGU_TS
    Hs, C = N_HEADS_SGU, SGU_CHUNK
    return pl.pallas_call(
        _sgu_kernel,
        out_shape=jax.ShapeDtypeStruct((B, S, D_SGU), F32),
        grid=(B, S // ts),
        in_specs=[pl.BlockSpec((1, ts, D_SGU), lambda b, i: (b, i, 0)),
                  pl.BlockSpec((1, ts, D_SGU), lambda b, i: (b, i, 1)),
                  pl.BlockSpec((Hs, C, C), lambda b, i: (0, 0, 0)),
                  pl.BlockSpec((Hs, C, HEAD_DIM), lambda b, i: (0, 0, 0)),
                  pl.BlockSpec((Hs, HEAD_DIM), lambda b, i: (0, 0)),
                  pl.BlockSpec((Hs, HEAD_DIM), lambda b, i: (0, 0))],
        out_specs=pl.BlockSpec((1, ts, D_SGU), lambda b, i: (b, i, 0)),
        compiler_params=_params(("arbitrary", "arbitrary")),
        name="sgu",
    )(uv, uv, sgu_w, sgu_b_lanes, ln_g, ln_b)


OUTPROJ_TM = 512


def _outproj_kernel(a_ref, s_ref, x_ref, w_ref, ga_ref, gs_ref, gt_ref, sc_ref, sh_ref, g2_ref,
                    h_ref, hn_ref, hnT_ref):
    ra = _rms(a_ref[0], RMS_EPS) * ga_ref[...]
    rs = _rms(s_ref[0], RMS_EPS) * gs_ref[...]
    mixed = jnp.concatenate([ra, rs], axis=1).astype(BF16)
    y = jnp.dot(mixed, w_ref[...], preferred_element_type=F32)
    h = x_ref[0] + gt_ref[0] * y
    h_ref[0] = h
    hn = _rms(h, RMS_EPS) * g2_ref[...]
    hn = hn * (1.0 + sc_ref[0]) + sh_ref[0]
    hn_ref[0] = hn.astype(BF16)
    hnT_ref[...] = hn.T.astype(BF16)


def _outproj(attn, sgu, x, w_out, g_attn, g_sgu, gt1, sc2, sh2, g2):
    B, S, D = x.shape
    tm = OUTPROJ_TM
    nt = S // tm
    mod = lambda: pl.BlockSpec((1, 1, D), lambda b, i: (b, 0, 0))
    return pl.pallas_call(
        _outproj_kernel,
        out_shape=(jax.ShapeDtypeStruct((B, S, D), F32),
                   jax.ShapeDtypeStruct((B, S, D), BF16),
                   jax.ShapeDtypeStruct((D, B * S), BF16)),
        grid=(B, nt),
        in_specs=[pl.BlockSpec((1, tm, D_ATTN), lambda b, i: (b, i, 0)),
                  pl.BlockSpec((1, tm, D_SGU), lambda b, i: (b, i, 0)),
                  pl.BlockSpec((1, tm, D), lambda b, i: (b, i, 0)),
                  pl.BlockSpec((D, D), lambda b, i: (0, 0)),
                  pl.BlockSpec((1, D_ATTN), lambda b, i: (0, 0)),
                  pl.BlockSpec((1, D_SGU), lambda b, i: (0, 0)),
                  mod(), mod(), mod(),
                  pl.BlockSpec((1, D), lambda b, i: (0, 0))],
        out_specs=(pl.BlockSpec((1, tm, D), lambda b, i: (b, i, 0)),
                   pl.BlockSpec((1, tm, D), lambda b, i: (b, i, 0)),
                   pl.BlockSpec((D, tm), lambda b, i: (0, b * nt + i))),
        compiler_params=_params(("arbitrary", "arbitrary")),
        name="outproj",
    )(attn, sgu, x, w_out, g_attn, g_sgu, gt1, sc2, sh2, g2)


PEERK_TM = 256
TOP_ROWS = 24
N_TOP = PEER_TOPK + 1


def _extract_top(s, count):
    tops = []
    for _ in range(count):
        m = jnp.max(s, axis=0, keepdims=True)
        tops.append(m)
        s = jnp.where(s == m, NEG_INF, s)
    return tops


def _peerk_kernel(hn_ref, wq_ref, sk_ref, thr_ref, e1_ref, s2_ref, e2_ref, q_s, s1_s, top_s):
    tm = hn_ref.shape[0]
    q_s[...] = jnp.dot(hn_ref[...], wq_ref[...], preferred_element_type=F32).astype(BF16)
    pad = jnp.full((TOP_ROWS - N_TOP, tm), NEG_INF, F32)

    def score_body(hp, carry):
        col0 = pl.multiple_of(hp * PEER_NKEYS, PEER_NKEYS)
        qh = q_s[:, pl.ds(col0, PEER_NKEYS)]
        s = lax.dot_general(sk_ref[hp], qh, (((1,), (1,)), ((), ())), preferred_element_type=F32)
        h = hp // 2

        @pl.when(hp % 2 == 0)
        def _():
            s1_s[h] = s

        @pl.when(hp % 2 == 1)
        def _():
            s2_ref[h] = s

        top_s[hp] = jnp.concatenate(_extract_top(s, N_TOP) + [pad], axis=0)
        return carry

    lax.fori_loop(0, 2 * PEER_HEADS, score_body, 0)

    def head_body(h, carry):
        a = top_s[2 * h]
        b = top_s[2 * h + 1]
        groups = [a[0:1] + b]
        groups += [a[i:i + 1] + b[0:SUBLANES] for i in range(1, SUBLANES)]
        groups += [a[SUBLANES:TOP_ROWS] + b[0:1]]
        best = _extract_top(jnp.concatenate(groups, axis=0), N_TOP)
        z = jnp.zeros_like(best[0])
        for v in best[:PEER_TOPK]:
            z = z + jnp.exp(v - best[0])
        tau = 0.5 * (best[PEER_TOPK - 1] + best[PEER_TOPK])
        s1 = s1_s[h]
        thr_ref[h] = tau - s1
        e1_ref[h] = jnp.exp(s1 - a[0:1]) / z
        e2_ref[h] = jnp.exp(s2_ref[h] - b[0:1])
        return carry

    lax.fori_loop(0, PEER_HEADS, head_body, 0)


def _peer_keys(hn2, w_q, sub_keys):
    T, D = hn2.shape
    tm = PEERK_TM
    H, K = PEER_HEADS, PEER_NKEYS
    out = jax.ShapeDtypeStruct((H, K, T), F32)
    ospec = lambda: pl.BlockSpec((H, K, tm), lambda i: (0, 0, i))
    return pl.pallas_call(
        _peerk_kernel,
        out_shape=(out, out, out, out),
        grid=(T // tm,),
        in_specs=[pl.BlockSpec((tm, D), lambda i: (i, 0)),
                  pl.BlockSpec((D, 2 * H * K), lambda i: (0, 0)),
                  pl.BlockSpec((2 * H, K, K), lambda i: (0, 0, 0))],
        out_specs=(ospec(), ospec(), ospec(), ospec()),
        scratch_shapes=[pltpu.VMEM((tm, 2 * H * K), BF16),
                        pltpu.VMEM((H, K, tm), F32),
                        pltpu.VMEM((2 * H, TOP_ROWS, tm), F32)],
        compiler_params=_params(("arbitrary",)),
        name="peer_keys",
    )(hn2, w_q, sub_keys)


PEER_TT = 512
PEER_TE = 1024


def _peer_kernel(hT_ref, u_ref, vt_ref, thr_ref, e1_ref, s2_ref, e2_ref, o_ref, a_s, p_s):
    j = pl.program_id(1)
    K = PEER_NKEYS
    a_s[...] = jnp.dot(u_ref[...], hT_ref[...], preferred_element_type=F32)
    for c in range(PEER_TE // K):
        gate = None
        for h in range(PEER_HEADS):
            picked = jnp.where(s2_ref[h] > thr_ref[h, c:c + 1, :], e2_ref[h], 0.0)
            term = e1_ref[h, c:c + 1, :] * picked
            gate = term if gate is None else gate + term
        p_s[c * K:(c + 1) * K, :] = (gate * _gelu(a_s[c * K:(c + 1) * K, :])).astype(BF16)
    contrib = jnp.dot(vt_ref[...], p_s[...], preferred_element_type=F32)

    @pl.when(j == 0)
    def _():
        o_ref[...] = contrib

    @pl.when(j > 0)
    def _():
        o_ref[...] += contrib


def _peer_ffn(hn2T, u, vT, thr, e1, s2, e2):
    D, T = hn2T.shape
    E = u.shape[0]
    tt, te = PEER_TT, PEER_TE
    H, K = PEER_HEADS, PEER_NKEYS
    per_c = pl.BlockSpec((H, te // K, tt), lambda i, j: (0, j, i))
    per_t = pl.BlockSpec((H, K, tt), lambda i, j: (0, 0, i))
    return pl.pallas_call(
        _peer_kernel,
        out_shape=jax.ShapeDtypeStruct((D, T), F32),
        grid=(T // tt, E // te),
        in_specs=[pl.BlockSpec((D, tt), lambda i, j: (0, i)),
                  pl.BlockSpec((te, D), lambda i, j: (j, 0)),
                  pl.BlockSpec((D, te), lambda i, j: (0, j)),
                  per_c, per_c, per_t, per_t],
        out_specs=pl.BlockSpec((D, tt), lambda i, j: (0, i)),
        scratch_shapes=[pltpu.VMEM((te, tt), F32),
                        pltpu.VMEM((te, tt), BF16)],
        compiler_params=_params(("arbitrary", "arbitrary")),
        name="peer_ffn",
    )(hn2T, u, vT, thr, e1, s2, e2)


FINAL_TM = 512


def _final_kernel(h_ref, pT_ref, gt_ref, g_ref, o_ref):
    h = h_ref[0] + gt_ref[0] * pT_ref[...].T
    o_ref[0] = _rms(h, RMS_EPS) * g_ref[...]


def _final(h1, peerT, gt2, g_final):
    B, S, D = h1.shape
    tm = FINAL_TM
    nt = S // tm
    return pl.pallas_call(
        _final_kernel,
        out_shape=jax.ShapeDtypeStruct((B, S, D), F32),
        grid=(B, nt),
        in_specs=[pl.BlockSpec((1, tm, D), lambda b, i: (b, i, 0)),
                  pl.BlockSpec((D, tm), lambda b, i: (0, b * nt + i)),
                  pl.BlockSpec((1, 1, D), lambda b, i: (b, 0, 0)),
                  pl.BlockSpec((1, D), lambda b, i: (0, 0))],
        out_specs=pl.BlockSpec((1, tm, D), lambda b, i: (b, i, 0)),
        compiler_params=_params(("arbitrary", "arbitrary")),
        name="final",
    )(h1, peerT, gt2, g_final)


def kernel(x, c, positions, w_ada, b_ada, g_norm1, w_in, g_attn_out, g_sgu_out, sgu_w, sgu_b,
           sgu_ln_g, sgu_ln_b, w_out, g_norm2, peer_w_q, peer_sub_keys, peer_u, peer_v, g_final):
    B, S, D = x.shape
    assert w_ada.shape[0] == 1, "single-layer block"
    l = 0
    half = HEAD_DIM // 2
    inv = ROPE_THETA ** (-jnp.arange(half, dtype=F32) / half)
    inv2 = jnp.concatenate([inv, inv]).reshape(1, HEAD_DIM)
    posf = positions.astype(F32).reshape(B, S, 1)
    row = lambda g: g.reshape(1, -1)

    mod = _adaln(c, w_ada[l], b_ada[l]).reshape(B, N_MOD, 1, D)
    sh1, sc1, gt1, sh2, sc2, gt2 = [mod[:, i] for i in range(N_MOD)]

    qkv, uv = _inproj(x, posf, inv2, sc1, sh1, row(g_norm1[l]), w_in[l].astype(BF16))
    attn = _attention(qkv)
    sgu_b_lanes = jnp.broadcast_to(sgu_b[l][:, :, None], (N_HEADS_SGU, SGU_CHUNK, HEAD_DIM))
    sgu = _sgu(uv, sgu_w[l], sgu_b_lanes, sgu_ln_g[l], sgu_ln_b[l])
    h1, hn2, hn2T = _outproj(attn, sgu, x, w_out[l].astype(BF16), row(g_attn_out[l]),
                             row(g_sgu_out[l]), gt1, sc2, sh2, row(g_norm2[l]))

    sk = peer_sub_keys[l].astype(BF16).reshape(2 * PEER_HEADS, PEER_NKEYS, -1)
    thr, e1, s2, e2 = _peer_keys(hn2.reshape(B * S, D), peer_w_q[l].astype(BF16), sk)
    peerT = _peer_ffn(hn2T, peer_u[l].astype(BF16), peer_v[l].T.astype(BF16), thr, e1, s2, e2)
    return _final(h1, peerT, gt2, row(g_final))
```

```python
import functools
import math

import jax
import jax.numpy as jnp
from jax import lax
from jax.experimental import pallas as pl
from jax.experimental.pallas import tpu as pltpu

F32 = jnp.float32
BF16 = jnp.bfloat16

HEAD_DIM = 128
N_HEADS_ATTN = 12
N_HEADS_SGU = 4
D_ATTN = N_HEADS_ATTN * HEAD_DIM
D_SGU = N_HEADS_SGU * HEAD_DIM
D_QKV = 3 * D_ATTN
DILATION_PATTERNS = ((128, 1), (512, 4), (2048, 16))
ATTN_STEPS = 128
ATTN_UNROLL = 8
ROPE_THETA = 10000.0
SGU_CHUNK = 128
PEER_HEADS = 8
PEER_NKEYS = 128
PEER_TOPK = 16
N_MOD = 6
RMS_EPS = 1e-6
LN_EPS = 1e-5

LANES = 128
SUBLANES = 8
VMEM_LIMIT = 56 * 1024 * 1024
NEG_INF = float("-inf")


def _params(semantics):
    return pltpu.CompilerParams(dimension_semantics=semantics, vmem_limit_bytes=VMEM_LIMIT)


def _gelu(x):
    return 0.5 * x * (1.0 + lax.erf(x * (1.0 / math.sqrt(2.0))))


def _rms(x, eps):
    return x * lax.rsqrt(jnp.mean(x * x, axis=-1, keepdims=True) + eps)


def _adaln_kernel(c_ref, w_ref, b_ref, o_ref):
    c = c_ref[...]
    cond = c * jax.nn.sigmoid(c)
    o_ref[...] = jnp.dot(cond.astype(BF16), w_ref[...].astype(BF16),
                         preferred_element_type=F32) + b_ref[...]


def _adaln(c, w_ada, b_ada):
    B, D = c.shape
    N = w_ada.shape[1]
    tn = 1024
    rows = SUBLANES
    c_pad = jnp.pad(c, ((0, rows - B), (0, 0)))
    out = pl.pallas_call(
        _adaln_kernel,
        out_shape=jax.ShapeDtypeStruct((rows, N), F32),
        grid=(N // tn,),
        in_specs=[pl.BlockSpec((rows, D), lambda j: (0, 0)),
                  pl.BlockSpec((D, tn), lambda j: (0, j)),
                  pl.BlockSpec((1, tn), lambda j: (0, j))],
        out_specs=pl.BlockSpec((rows, tn), lambda j: (0, j)),
        compiler_params=_params(("arbitrary",)),
        name="adaln",
    )(c_pad, w_ada, b_ada.reshape(1, N))
    return out[:B]


INPROJ_TM = 1024
INPROJ_TN = 512
_QK_TILES = 2 * D_ATTN // INPROJ_TN
_Q_TILES = D_ATTN // INPROJ_TN
_QKV_TILES = D_QKV // INPROJ_TN


def _inproj_kernel(x_ref, pos_ref, inv_ref, sc_ref, sh_ref, g_ref, w_ref,
                   qkv_ref, uv_ref, hn_ref, cos_ref, sin_ref):
    j = pl.program_id(2)

    @pl.when(j == 0)
    def _():
        x = x_ref[0]
        hn = _rms(x, RMS_EPS) * g_ref[...]
        hn = hn * (1.0 + sc_ref[0]) + sh_ref[0]
        hn_ref[...] = hn.astype(BF16)
        ang = pos_ref[0] * inv_ref[...]
        lane = lax.broadcasted_iota(jnp.int32, ang.shape, 1)
        cos_ref[...] = jnp.cos(ang)
        sin_ref[...] = jnp.where(lane < HEAD_DIM // 2, -1.0, 1.0) * jnp.sin(ang)

    acc = jnp.dot(hn_ref[...], w_ref[...], preferred_element_type=F32)

    @pl.when(j < _QK_TILES)
    def _():
        scale = jnp.where(j < _Q_TILES, HEAD_DIM ** -0.5, 1.0).astype(F32)
        cos = cos_ref[...]
        sin = sin_ref[...]
        for h in range(INPROJ_TN // HEAD_DIM):
            t = acc[:, h * HEAD_DIM:(h + 1) * HEAD_DIM]
            r = t * cos + pltpu.roll(t, HEAD_DIM // 2, 1) * sin
            qkv_ref[0, :, h * HEAD_DIM:(h + 1) * HEAD_DIM] = (r * scale).astype(BF16)

    @pl.when(jnp.logical_and(j >= _QK_TILES, j < _QKV_TILES))
    def _():
        qkv_ref[0] = acc.astype(BF16)

    @pl.when(j >= _QKV_TILES)
    def _():
        uv_ref[0] = _gelu(acc)


def _inproj(x, posf, inv2, sc1, sh1, g1, w_in):
    B, S, D = x.shape
    N = w_in.shape[1]
    tm, tn = INPROJ_TM, INPROJ_TN
    last_qkv = _QKV_TILES - 1
    return pl.pallas_call(
        _inproj_kernel,
        out_shape=(jax.ShapeDtypeStruct((B, S, D_QKV), BF16),
                   jax.ShapeDtypeStruct((B, S, 2 * D_SGU), F32)),
        grid=(B, S // tm, N // tn),
        in_specs=[pl.BlockSpec((1, tm, D), lambda b, i, j: (b, i, 0)),
                  pl.BlockSpec((1, tm, 1), lambda b, i, j: (b, i, 0)),
                  pl.BlockSpec((1, LANES), lambda b, i, j: (0, 0)),
                  pl.BlockSpec((1, 1, D), lambda b, i, j: (b, 0, 0)),
                  pl.BlockSpec((1, 1, D), lambda b, i, j: (b, 0, 0)),
                  pl.BlockSpec((1, D), lambda b, i, j: (0, 0)),
                  pl.BlockSpec((D, tn), lambda b, i, j: (0, j))],
        out_specs=(pl.BlockSpec((1, tm, tn), lambda b, i, j: (b, i, jnp.minimum(j, last_qkv))),
                   pl.BlockSpec((1, tm, tn), lambda b, i, j: (b, i, jnp.maximum(j - _QKV_TILES, 0)))),
        scratch_shapes=[pltpu.VMEM((tm, D), BF16),
                        pltpu.VMEM((tm, LANES), F32),
                        pltpu.VMEM((tm, LANES), F32)],
        compiler_params=_params(("arbitrary", "arbitrary", "arbitrary")),
        name="inproj",
    )(x, posf, inv2, sc1, sh1, g1, w_in)


def _attn_kernel(q_ref, k_ref, v_ref, o_ref, qf, kf, vf, qd, kd, vd, acc, m_s, l_s):
    S = q_ref.shape[1]
    steps = ATTN_STEPS
    n_blocks = S // steps
    qf[...] = q_ref[0].astype(F32)
    kf[...] = k_ref[0].astype(F32)
    vf[...] = v_ref[0].astype(F32)
    kd[0:steps, :] = jnp.zeros((steps, HEAD_DIM), BF16)
    vd[0:steps, :] = jnp.zeros((steps, HEAD_DIM), BF16)

    qi = lax.broadcasted_iota(jnp.int32, (steps, 2 * steps), 0)
    km = lax.broadcasted_iota(jnp.int32, (steps, 2 * steps), 1)
    band = jnp.logical_and(km >= qi, km <= qi + steps)

    for pi, (window, d) in enumerate(DILATION_PATTERNS):
        assert window // d == steps
        L = S // d
        nb = L // steps
        for r in range(d):
            qd[r * L:(r + 1) * L, :] = qf[pl.ds(r, L, stride=d), :].astype(BF16)
            kd[steps + r * L:steps + (r + 1) * L, :] = kf[pl.ds(r, L, stride=d), :].astype(BF16)
            vd[steps + r * L:steps + (r + 1) * L, :] = vf[pl.ds(r, L, stride=d), :].astype(BF16)

        def body(g, carry, pi=pi, d=d, nb=nb):
            n = g % nb
            r = g // nb
            row0 = pl.multiple_of(g * steps, steps)
            qb = qd[pl.ds(row0, steps), :]
            kc = kd[pl.ds(row0, 2 * steps), :]
            vc = vd[pl.ds(row0, 2 * steps), :]
            s = lax.dot_general(qb, kc, (((1,), (1,)), ((), ())), preferred_element_type=F32)
            first_key = jnp.where(n > 0, 0, steps)
            s = jnp.where(jnp.logical_and(band, km >= first_key), s, NEG_INF)
            mb = jnp.max(s, axis=-1, keepdims=True)
            p = jnp.exp(s - mb)
            lb = jnp.sum(p, axis=-1, keepdims=True)
            ob = jnp.dot(p.astype(BF16), vc, preferred_element_type=F32)
            rows = pl.ds(n * (steps * d) + r, steps, stride=d)
            mb_b = jnp.broadcast_to(mb, (steps, HEAD_DIM))
            lb_b = jnp.broadcast_to(lb, (steps, HEAD_DIM))
            if pi == 0:
                acc[rows, :] = ob
                m_s[rows, :] = mb_b
                l_s[rows, :] = lb_b
            else:
                m_old = m_s[rows, :]
                m_new = jnp.maximum(m_old, mb_b)
                a_old = jnp.exp(m_old - m_new)
                a_blk = jnp.exp(mb_b - m_new)
                acc[rows, :] = acc[rows, :] * a_old + ob * a_blk
                l_s[rows, :] = l_s[rows, :] * a_old + lb_b * a_blk
                m_s[rows, :] = m_new
            return carry

        lax.fori_loop(0, n_blocks, body, 0, unroll=ATTN_UNROLL)

    o_ref[0] = acc[...] / l_s[...]


def _attention(qkv):
    B, S, _ = qkv.shape
    H = N_HEADS_ATTN
    blk = lambda off: pl.BlockSpec((1, S, HEAD_DIM), lambda b, h: (b, 0, off + h))
    return pl.pallas_call(
        _attn_kernel,
        out_shape=jax.ShapeDtypeStruct((B, S, D_ATTN), F32),
        grid=(B, H),
        in_specs=[blk(0), blk(H), blk(2 * H)],
        out_specs=pl.BlockSpec((1, S, HEAD_DIM), lambda b, h: (b, 0, h)),
        scratch_shapes=[pltpu.VMEM((S, HEAD_DIM), F32),
                        pltpu.VMEM((S, HEAD_DIM), F32),
                        pltpu.VMEM((S, HEAD_DIM), F32),
                        pltpu.VMEM((S, HEAD_DIM), BF16),
                        pltpu.VMEM((S + ATTN_STEPS, HEAD_DIM), BF16),
                        pltpu.VMEM((S + ATTN_STEPS, HEAD_DIM), BF16),
                        pltpu.VMEM((S, HEAD_DIM), F32),
                        pltpu.VMEM((S, HEAD_DIM), F32),
                        pltpu.VMEM((S, HEAD_DIM), F32)],
        compiler_params=_params(("arbitrary", "arbitrary")),
        name="attn",
    )(qkv, qkv, qkv)


SGU_TS = 512


def _sgu_kernel(u_ref, v_ref, w_ref, b_ref, lg_ref, lb_ref, o_ref):
    C = SGU_CHUNK
    n_chunks = u_ref.shape[1] // C
    row = lax.broadcasted_iota(jnp.int32, (C, C), 0)
    col = lax.broadcasted_iota(jnp.int32, (C, C), 1)
    for h in range(N_HEADS_SGU):
        cols = slice(h * HEAD_DIM, (h + 1) * HEAD_DIM)
        v = v_ref[0, :, cols]
        mu = jnp.mean(v, axis=-1, keepdims=True)
        vc = v - mu
        var = jnp.mean(vc * vc, axis=-1, keepdims=True)
        vn = vc * lax.rsqrt(var + LN_EPS) * lg_ref[h:h + 1, :] + lb_ref[h:h + 1, :]
        vn = vn.astype(BF16)
        ws = jnp.where(row >= col, w_ref[h], 0.0).astype(BF16)
        rhs = jnp.concatenate([vn[n * C:(n + 1) * C, :] for n in range(n_chunks)], axis=1)
        mixed = jnp.dot(ws, rhs, preferred_element_type=F32)
        for n in range(n_chunks):
            gate = mixed[:, n * HEAD_DIM:(n + 1) * HEAD_DIM] + b_ref[h]
            o_ref[0, n * C:(n + 1) * C, cols] = u_ref[0, n * C:(n + 1) * C, cols] * gate


def _sgu(uv, sgu_w, sgu_b_lanes, ln_g, ln_b):
    B, S, _ = uv.shape
    ts = SGU_TS
    Hs, C = N_HEADS_SGU, SGU_CHUNK
    return pl.pallas_call(
        _sgu_kernel,
        out_shape=jax.ShapeDtypeStruct((B, S, D_SGU), F32),
        grid=(B, S // ts),
        in_specs=[pl.BlockSpec((1, ts, D_SGU), lambda b, i: (b, i, 0)),
                  pl.BlockSpec((1, ts, D_SGU), lambda b, i: (b, i, 1)),
                  pl.BlockSpec((Hs, C, C), lambda b, i: (0, 0, 0)),
                  pl.BlockSpec((Hs, C, HEAD_DIM), lambda b, i: (0, 0, 0)),
                  pl.BlockSpec((Hs, HEAD_DIM), lambda b, i: (0, 0)),
                  pl.BlockSpec((Hs, HEAD_DIM), lambda b, i: (0, 0))],
        out_specs=pl.BlockSpec((1, ts, D_SGU), lambda b, i: (b, i, 0)),
        compiler_params=_params(("arbitrary", "arbitrary")),
        name="sgu",
    )(uv, uv, sgu_w, sgu_b_lanes, ln_g, ln_b)


OUTPROJ_TM = 512


def _outproj_kernel(a_ref, s_ref, x_ref, w_ref, ga_ref, gs_ref, gt_ref, sc_ref, sh_ref, g2_ref,
                    h_ref, hn_ref, hnT_ref):
    ra = _rms(a_ref[0], RMS_EPS) * ga_ref[...]
    rs = _rms(s_ref[0], RMS_EPS) * gs_ref[...]
    mixed = jnp.concatenate([ra, rs], axis=1).astype(BF16)
    y = jnp.dot(mixed, w_ref[...], preferred_element_type=F32)
    h = x_ref[0] + gt_ref[0] * y
    h_ref[0] = h
    hn = _rms(h, RMS_EPS) * g2_ref[...]
    hn = hn * (1.0 + sc_ref[0]) + sh_ref[0]
    hn_ref[0] = hn.astype(BF16)
    hnT_ref[...] = hn.T.astype(BF16)


def _outproj(attn, sgu, x, w_out, g_attn, g_sgu, gt1, sc2, sh2, g2):
    B, S, D = x.shape
    tm = OUTPROJ_TM
    nt = S // tm
    mod = lambda: pl.BlockSpec((1, 1, D), lambda b, i: (b, 0, 0))
    return pl.pallas_call(
        _outproj_kernel,
        out_shape=(jax.ShapeDtypeStruct((B, S, D), F32),
                   jax.ShapeDtypeStruct((B, S, D), BF16),
                   jax.ShapeDtypeStruct((D, B * S), BF16)),
        grid=(B, nt),
        in_specs=[pl.BlockSpec((1, tm, D_ATTN), lambda b, i: (b, i, 0)),
                  pl.BlockSpec((1, tm, D_SGU), lambda b, i: (b, i, 0)),
                  pl.BlockSpec((1, tm, D), lambda b, i: (b, i, 0)),
                  pl.BlockSpec((D, D), lambda b, i: (0, 0)),
                  pl.BlockSpec((1, D_ATTN), lambda b, i: (0, 0)),
                  pl.BlockSpec((1, D_SGU), lambda b, i: (0, 0)),
                  mod(), mod(), mod(),
                  pl.BlockSpec((1, D), lambda b, i: (0, 0))],
        out_specs=(pl.BlockSpec((1, tm, D), lambda b, i: (b, i, 0)),
                   pl.BlockSpec((1, tm, D), lambda b, i: (b, i, 0)),
                   pl.BlockSpec((D, tm), lambda b, i: (0, b * nt + i))),
        compiler_params=_params(("arbitrary", "arbitrary")),
        name="outproj",
    )(attn, sgu, x, w_out, g_attn, g_sgu, gt1, sc2, sh2, g2)


PEERK_TM = 256
TOP_ROWS = 24
N_TOP = PEER_TOPK + 1


def _extract_top(s, count):
    tops = []
    for _ in range(count):
        m = jnp.max(s, axis=0, keepdims=True)
        tops.append(m)
        s = jnp.where(s == m, NEG_INF, s)
    return tops


def _peerk_kernel(hn_ref, wq_ref, sk_ref, thr_ref, e1_ref, s2_ref, e2_ref, q_s, s1_s, top_s):
    tm = hn_ref.shape[0]
    q_s[...] = jnp.dot(hn_ref[...], wq_ref[...], preferred_element_type=F32).astype(BF16)
    pad = jnp.full((TOP_ROWS - N_TOP, tm), NEG_INF, F32)

    def score_body(hp, carry):
        col0 = pl.multiple_of(hp * PEER_NKEYS, PEER_NKEYS)
        qh = q_s[:, pl.ds(col0, PEER_NKEYS)]
        s = lax.dot_general(sk_ref[hp], qh, (((1,), (1,)), ((), ())), preferred_element_type=F32)
        h = hp // 2

        @pl.when(hp % 2 == 0)
        def _():
            s1_s[h] = s

        @pl.when(hp % 2 == 1)
        def _():
            s2_ref[h] = s

        top_s[hp] = jnp.concatenate(_extract_top(s, N_TOP) + [pad], axis=0)
        return carry

    lax.fori_loop(0, 2 * PEER_HEADS, score_body, 0, unroll=4)

    def head_body(h, carry):
        a = top_s[2 * h]
        b = top_s[2 * h + 1]
        groups = [a[0:1] + b]
        groups += [a[i:i + 1] + b[0:SUBLANES] for i in range(1, SUBLANES)]
        groups += [a[SUBLANES:TOP_ROWS] + b[0:1]]
        best = _extract_top(jnp.concatenate(groups, axis=0), N_TOP)
        z = jnp.zeros_like(best[0])
        for v in best[:PEER_TOPK]:
            z = z + jnp.exp(v - best[0])
        tau = 0.5 * (best[PEER_TOPK - 1] + best[PEER_TOPK])
        s1 = s1_s[h]
        thr_ref[h] = tau - s1
        e1_ref[h] = jnp.exp(s1 - a[0:1]) / z
        e2_ref[h] = jnp.exp(s2_ref[h] - b[0:1])
        return carry

    lax.fori_loop(0, PEER_HEADS, head_body, 0, unroll=4)


def _peer_keys(hn2, w_q, sub_keys):
    T, D = hn2.shape
    tm = PEERK_TM
    H, K = PEER_HEADS, PEER_NKEYS
    out = jax.ShapeDtypeStruct((H, K, T), F32)
    ospec = lambda: pl.BlockSpec((H, K, tm), lambda i: (0, 0, i))
    return pl.pallas_call(
        _peerk_kernel,
        out_shape=(out, out, out, out),
        grid=(T // tm,),
        in_specs=[pl.BlockSpec((tm, D), lambda i: (i, 0)),
                  pl.BlockSpec((D, 2 * H * K), lambda i: (0, 0)),
                  pl.BlockSpec((2 * H, K, K), lambda i: (0, 0, 0))],
        out_specs=(ospec(), ospec(), ospec(), ospec()),
        scratch_shapes=[pltpu.VMEM((tm, 2 * H * K), BF16),
                        pltpu.VMEM((H, K, tm), F32),
                        pltpu.VMEM((2 * H, TOP_ROWS, tm), F32)],
        compiler_params=_params(("arbitrary",)),
        name="peer_keys",
    )(hn2, w_q, sub_keys)


PEER_TT = 512
PEER_TE = 1024
PEER_PIECES = 4
PEER_GATE_ROWS = 16


def _peer_gate(a, c_range, thr_ref, e1_ref, s2_ref, e2_ref):
    K = PEER_NKEYS
    R = PEER_GATE_ROWS
    out_rows = []
    for ci, c in enumerate(c_range):
        for r in range(K // R):
            keys = slice(r * R, (r + 1) * R)
            tiles = []
            for l in range(PEER_TT // LANES):
                lanes = slice(l * LANES, (l + 1) * LANES)
                gate = None
                for h in range(PEER_HEADS):
                    picked = jnp.where(s2_ref[h, keys, lanes] > thr_ref[h, c:c + 1, lanes],
                                       e2_ref[h, keys, lanes], 0.0)
                    term = e1_ref[h, c:c + 1, lanes] * picked
                    gate = term if gate is None else gate + term
                act = _gelu(a[ci * K + r * R:ci * K + (r + 1) * R, lanes])
                tiles.append((gate * act).astype(BF16))
            out_rows.append(jnp.concatenate(tiles, axis=1))
    return jnp.concatenate(out_rows, axis=0)


def _peer_kernel(hT_ref, u_ref, vt_ref, thr_ref, e1_ref, s2_ref, e2_ref, o_ref, p0, p1,
                 *, chunks_per_tile):
    g = pl.program_id(0)

    @pl.when(g == 0)
    def _():
        p1[...] = jnp.zeros(p1.shape, p1.dtype)

    @pl.when(jnp.logical_or(g == 0, (g + chunks_per_tile - 1) % chunks_per_tile == 0))
    def _():
        o_ref[...] = jnp.zeros(o_ref.shape, o_ref.dtype)

    def step(p_w, p_r):
        n_c = PEER_TE // PEER_NKEYS
        d_rows = o_ref.shape[0] // PEER_PIECES
        e_rows = PEER_TE // PEER_PIECES
        for q in range(PEER_PIECES):
            er = slice(q * e_rows, (q + 1) * e_rows)
            dr = slice(q * d_rows, (q + 1) * d_rows)
            a = jnp.dot(u_ref[er, :], hT_ref[...], preferred_element_type=F32)
            p_new = _peer_gate(a, range(q * n_c // PEER_PIECES, (q + 1) * n_c // PEER_PIECES),
                               thr_ref, e1_ref, s2_ref, e2_ref)
            o_new = o_ref[dr, :] + jnp.dot(vt_ref[dr, :], p_r[...], preferred_element_type=F32)
            p_w[er, :] = p_new
            o_ref[dr, :] = o_new

    @pl.when(g % 2 == 0)
    def _():
        step(p0, p1)

    @pl.when(g % 2 == 1)
    def _():
        step(p1, p0)


def _peer_ffn(hn2T, u, vT, thr, e1, s2, e2):
    D, T = hn2T.shape
    E = u.shape[0]
    tt, te = PEER_TT, PEER_TE
    H, K = PEER_HEADS, PEER_NKEYS
    nj = E // te
    n_chunks = (T // tt) * nj
    act = lambda g: jnp.minimum(g, n_chunks - 1)
    val = lambda g: jnp.maximum(g - 1, 0)
    per_c = pl.BlockSpec((H, te // K, tt), lambda g: (0, act(g) % nj, act(g) // nj))
    per_t = pl.BlockSpec((H, K, tt), lambda g: (0, 0, act(g) // nj))
    return pl.pallas_call(
        functools.partial(_peer_kernel, chunks_per_tile=nj),
        out_shape=jax.ShapeDtypeStruct((D, T), F32),
        grid=(n_chunks + 1,),
        in_specs=[pl.BlockSpec((D, tt), lambda g: (0, act(g) // nj)),
                  pl.BlockSpec((te, D), lambda g: (act(g) % nj, 0)),
                  pl.BlockSpec((D, te), lambda g: (0, val(g) % nj)),
                  per_c, per_c, per_t, per_t],
        out_specs=pl.BlockSpec((D, tt), lambda g: (0, val(g) // nj)),
        scratch_shapes=[pltpu.VMEM((te, tt), BF16),
                        pltpu.VMEM((te, tt), BF16)],
        compiler_params=_params(("arbitrary",)),
        name="peer_ffn",
    )(hn2T, u, vT, thr, e1, s2, e2)


FINAL_TM = 512


def _final_kernel(h_ref, pT_ref, gt_ref, g_ref, o_ref):
    h = h_ref[0] + gt_ref[0] * pT_ref[...].T
    o_ref[0] = _rms(h, RMS_EPS) * g_ref[...]


def _final(h1, peerT, gt2, g_final):
    B, S, D = h1.shape
    tm = FINAL_TM
    nt = S // tm
    return pl.pallas_call(
        _final_kernel,
        out_shape=jax.ShapeDtypeStruct((B, S, D), F32),
        grid=(B, nt),
        in_specs=[pl.BlockSpec((1, tm, D), lambda b, i: (b, i, 0)),
                  pl.BlockSpec((D, tm), lambda b, i: (0, b * nt + i)),
                  pl.BlockSpec((1, 1, D), lambda b, i: (b, 0, 0)),
                  pl.BlockSpec((1, D), lambda b, i: (0, 0))],
        out_specs=pl.BlockSpec((1, tm, D), lambda b, i: (b, i, 0)),
        compiler_params=_params(("arbitrary", "arbitrary")),
        name="final",
    )(h1, peerT, gt2, g_final)


def kernel(x, c, positions, w_ada, b_ada, g_norm1, w_in, g_attn_out, g_sgu_out, sgu_w, sgu_b,
           sgu_ln_g, sgu_ln_b, w_out, g_norm2, peer_w_q, peer_sub_keys, peer_u, peer_v, g_final):
    B, S, D = x.shape
    assert w_ada.shape[0] == 1, "single-layer block"
    l = 0
    half = HEAD_DIM // 2
    inv = ROPE_THETA ** (-jnp.arange(half, dtype=F32) / half)
    inv2 = jnp.concatenate([inv, inv]).reshape(1, HEAD_DIM)
    posf = positions.astype(F32).reshape(B, S, 1)
    row = lambda g: g.reshape(1, -1)

    mod = _adaln(c, w_ada[l], b_ada[l]).reshape(B, N_MOD, 1, D)
    sh1, sc1, gt1, sh2, sc2, gt2 = [mod[:, i] for i in range(N_MOD)]

    qkv, uv = _inproj(x, posf, inv2, sc1, sh1, row(g_norm1[l]), w_in[l].astype(BF16))
    attn = _attention(qkv)
    sgu_b_lanes = jnp.broadcast_to(sgu_b[l][:, :, None], (N_HEADS_SGU, SGU_CHUNK, HEAD_DIM))
    sgu = _sgu(uv, sgu_w[l], sgu_b_lanes, sgu_ln_g[l], sgu_ln_b[l])
    h1, hn2, hn2T = _outproj(attn, sgu, x, w_out[l].astype(BF16), row(g_attn_out[l]),
                             row(g_sgu_out[l]), gt1, sc2, sh2, row(g_norm2[l]))

    sk = peer_sub_keys[l].astype(BF16).reshape(2 * PEER_HEADS, PEER_NKEYS, -1)
    thr, e1, s2, e2 = _peer_keys(hn2.reshape(B * S, D), peer_w_q[l].astype(BF16), sk)
    peerT = _peer_ffn(hn2T, peer_u[l].astype(BF16), peer_v[l].T.astype(BF16), thr, e1, s2, e2)
    return _final(h1, peerT, gt2, row(g_final))
```

```python
import functools
import math

import jax
import jax.numpy as jnp
from jax import lax
from jax.experimental import pallas as pl
from jax.experimental.pallas import tpu as pltpu

F32 = jnp.float32
BF16 = jnp.bfloat16

HEAD_DIM = 128
N_HEADS_ATTN = 12
N_HEADS_SGU = 4
D_ATTN = N_HEADS_ATTN * HEAD_DIM
D_SGU = N_HEADS_SGU * HEAD_DIM
D_QKV = 3 * D_ATTN
DILATION_PATTERNS = ((128, 1), (512, 4), (2048, 16))
ATTN_STEPS = 128
ATTN_UNROLL = 8
ROPE_THETA = 10000.0
SGU_CHUNK = 128
PEER_HEADS = 8
PEER_NKEYS = 128
PEER_TOPK = 16
N_MOD = 6
RMS_EPS = 1e-6
LN_EPS = 1e-5

LANES = 128
SUBLANES = 8
VMEM_LIMIT = 56 * 1024 * 1024
NEG_INF = float("-inf")


def _params(semantics):
    return pltpu.CompilerParams(dimension_semantics=semantics, vmem_limit_bytes=VMEM_LIMIT)


def _gelu(x):
    return 0.5 * x * (1.0 + lax.erf(x * (1.0 / math.sqrt(2.0))))


def _rms(x, eps):
    return x * lax.rsqrt(jnp.mean(x * x, axis=-1, keepdims=True) + eps)


def _adaln_kernel(c_ref, w_ref, b_ref, o_ref):
    c = c_ref[...]
    cond = c * jax.nn.sigmoid(c)
    o_ref[...] = jnp.dot(cond.astype(BF16), w_ref[...].astype(BF16),
                         preferred_element_type=F32) + b_ref[...]


def _adaln(c, w_ada, b_ada):
    B, D = c.shape
    N = w_ada.shape[1]
    tn = 1024
    rows = SUBLANES
    c_pad = jnp.pad(c, ((0, rows - B), (0, 0)))
    out = pl.pallas_call(
        _adaln_kernel,
        out_shape=jax.ShapeDtypeStruct((rows, N), F32),
        grid=(N // tn,),
        in_specs=[pl.BlockSpec((rows, D), lambda j: (0, 0)),
                  pl.BlockSpec((D, tn), lambda j: (0, j)),
                  pl.BlockSpec((1, tn), lambda j: (0, j))],
        out_specs=pl.BlockSpec((rows, tn), lambda j: (0, j)),
        compiler_params=_params(("arbitrary",)),
        name="adaln",
    )(c_pad, w_ada, b_ada.reshape(1, N))
    return out[:B]


INPROJ_TM = 1024
INPROJ_TN = 512
_QK_TILES = 2 * D_ATTN // INPROJ_TN
_Q_TILES = D_ATTN // INPROJ_TN
_QKV_TILES = D_QKV // INPROJ_TN


def _inproj_kernel(x_ref, pos_ref, inv_ref, sc_ref, sh_ref, g_ref, w_ref,
                   qkv_ref, uv_ref, hn_ref, cos_ref, sin_ref):
    j = pl.program_id(2)

    @pl.when(j == 0)
    def _():
        x = x_ref[0]
        hn = _rms(x, RMS_EPS) * g_ref[...]
        hn = hn * (1.0 + sc_ref[0]) + sh_ref[0]
        hn_ref[...] = hn.astype(BF16)
        ang = pos_ref[0] * inv_ref[...]
        lane = lax.broadcasted_iota(jnp.int32, ang.shape, 1)
        cos_ref[...] = jnp.cos(ang)
        sin_ref[...] = jnp.where(lane < HEAD_DIM // 2, -1.0, 1.0) * jnp.sin(ang)

    acc = jnp.dot(hn_ref[...], w_ref[...], preferred_element_type=F32)

    @pl.when(j < _QK_TILES)
    def _():
        scale = jnp.where(j < _Q_TILES, HEAD_DIM ** -0.5, 1.0).astype(F32)
        cos = cos_ref[...]
        sin = sin_ref[...]
        for h in range(INPROJ_TN // HEAD_DIM):
            t = acc[:, h * HEAD_DIM:(h + 1) * HEAD_DIM]
            r = t * cos + pltpu.roll(t, HEAD_DIM // 2, 1) * sin
            qkv_ref[0, :, h * HEAD_DIM:(h + 1) * HEAD_DIM] = (r * scale).astype(BF16)

    @pl.when(jnp.logical_and(j >= _QK_TILES, j < _QKV_TILES))
    def _():
        qkv_ref[0] = acc.astype(BF16)

    @pl.when(j >= _QKV_TILES)
    def _():
        uv_ref[0] = _gelu(acc)


def _inproj(x, posf, inv2, sc1, sh1, g1, w_in):
    B, S, D = x.shape
    N = w_in.shape[1]
    tm, tn = INPROJ_TM, INPROJ_TN
    last_qkv = _QKV_TILES - 1
    return pl.pallas_call(
        _inproj_kernel,
        out_shape=(jax.ShapeDtypeStruct((B, S, D_QKV), BF16),
                   jax.ShapeDtypeStruct((B, S, 2 * D_SGU), F32)),
        grid=(B, S // tm, N // tn),
        in_specs=[pl.BlockSpec((1, tm, D), lambda b, i, j: (b, i, 0)),
                  pl.BlockSpec((1, tm, 1), lambda b, i, j: (b, i, 0)),
                  pl.BlockSpec((1, LANES), lambda b, i, j: (0, 0)),
                  pl.BlockSpec((1, 1, D), lambda b, i, j: (b, 0, 0)),
                  pl.BlockSpec((1, 1, D), lambda b, i, j: (b, 0, 0)),
                  pl.BlockSpec((1, D), lambda b, i, j: (0, 0)),
                  pl.BlockSpec((D, tn), lambda b, i, j: (0, j))],
        out_specs=(pl.BlockSpec((1, tm, tn), lambda b, i, j: (b, i, jnp.minimum(j, last_qkv))),
                   pl.BlockSpec((1, tm, tn), lambda b, i, j: (b, i, jnp.maximum(j - _QKV_TILES, 0)))),
        scratch_shapes=[pltpu.VMEM((tm, D), BF16),
                        pltpu.VMEM((tm, LANES), F32),
                        pltpu.VMEM((tm, LANES), F32)],
        compiler_params=_params(("arbitrary", "arbitrary", "arbitrary")),
        name="inproj",
    )(x, posf, inv2, sc1, sh1, g1, w_in)


def _attn_kernel(q_ref, k_ref, v_ref, o_ref, qf, kf, vf, qd, kd, vd, acc, m_s, l_s):
    S = q_ref.shape[1]
    steps = ATTN_STEPS
    n_blocks = S // steps
    qf[...] = q_ref[0].astype(F32)
    kf[...] = k_ref[0].astype(F32)
    vf[...] = v_ref[0].astype(F32)
    kd[0:steps, :] = jnp.zeros((steps, HEAD_DIM), BF16)
    vd[0:steps, :] = jnp.zeros((steps, HEAD_DIM), BF16)

    qi = lax.broadcasted_iota(jnp.int32, (steps, 2 * steps), 0)
    km = lax.broadcasted_iota(jnp.int32, (steps, 2 * steps), 1)
    band = jnp.logical_and(km >= qi, km <= qi + steps)

    for pi, (window, d) in enumerate(reversed(DILATION_PATTERNS)):
        assert window // d == steps
        L = S // d
        nb = L // steps
        if d == 1:
            qd[...] = q_ref[0]
            kd[steps:, :] = k_ref[0]
            vd[steps:, :] = v_ref[0]
        else:
            for r in range(d):
                qd[r * L:(r + 1) * L, :] = qf[pl.ds(r, L, stride=d), :].astype(BF16)
                kd[steps + r * L:steps + (r + 1) * L, :] = kf[pl.ds(r, L, stride=d), :].astype(BF16)
                vd[steps + r * L:steps + (r + 1) * L, :] = vf[pl.ds(r, L, stride=d), :].astype(BF16)

        def body(g, carry, pi=pi, d=d, nb=nb):
            n = g % nb
            r = g // nb
            row0 = pl.multiple_of(g * steps, steps)
            qb = qd[pl.ds(row0, steps), :]
            kc = kd[pl.ds(row0, 2 * steps), :]
            vc = vd[pl.ds(row0, 2 * steps), :]
            s = lax.dot_general(qb, kc, (((1,), (1,)), ((), ())), preferred_element_type=F32)
            first_key = jnp.where(n > 0, 0, steps)
            s = jnp.where(jnp.logical_and(band, km >= first_key), s, NEG_INF)
            mb = jnp.max(s, axis=-1, keepdims=True)
            p = jnp.exp(s - mb)
            lb = jnp.sum(p, axis=-1, keepdims=True)
            ob = jnp.dot(p.astype(BF16), vc, preferred_element_type=F32)
            rows = pl.ds(n * (steps * d) + r, steps, stride=d)
            mb_b = jnp.broadcast_to(mb, (steps, HEAD_DIM))
            lb_b = jnp.broadcast_to(lb, (steps, HEAD_DIM))
            if pi == 0:
                acc[rows, :] = ob
                m_s[rows, :] = mb_b
                l_s[rows, :] = lb_b
            else:
                m_old = m_s[rows, :]
                m_new = jnp.maximum(m_old, mb_b)
                a_old = jnp.exp(m_old - m_new)
                a_blk = jnp.exp(mb_b - m_new)
                acc[rows, :] = acc[rows, :] * a_old + ob * a_blk
                l_s[rows, :] = l_s[rows, :] * a_old + lb_b * a_blk
                m_s[rows, :] = m_new
            return carry

        lax.fori_loop(0, n_blocks, body, 0, unroll=ATTN_UNROLL)

    o_ref[0] = acc[...] / l_s[...]


def _attention(qkv):
    B, S, _ = qkv.shape
    H = N_HEADS_ATTN
    blk = lambda off: pl.BlockSpec((1, S, HEAD_DIM), lambda b, h: (b, 0, off + h))
    return pl.pallas_call(
        _attn_kernel,
        out_shape=jax.ShapeDtypeStruct((B, S, D_ATTN), F32),
        grid=(B, H),
        in_specs=[blk(0), blk(H), blk(2 * H)],
        out_specs=pl.BlockSpec((1, S, HEAD_DIM), lambda b, h: (b, 0, h)),
        scratch_shapes=[pltpu.VMEM((S, HEAD_DIM), F32),
                        pltpu.VMEM((S, HEAD_DIM), F32),
                        pltpu.VMEM((S, HEAD_DIM), F32),
                        pltpu.VMEM((S, HEAD_DIM), BF16),
                        pltpu.VMEM((S + ATTN_STEPS, HEAD_DIM), BF16),
                        pltpu.VMEM((S + ATTN_STEPS, HEAD_DIM), BF16),
                        pltpu.VMEM((S, HEAD_DIM), F32),
                        pltpu.VMEM((S, HEAD_DIM), F32),
                        pltpu.VMEM((S, HEAD_DIM), F32)],
        compiler_params=_params(("arbitrary", "arbitrary")),
        name="attn",
    )(qkv, qkv, qkv)


SGU_TS = 512


def _sgu_kernel(u_ref, v_ref, w_ref, b_ref, lg_ref, lb_ref, o_ref):
    C = SGU_CHUNK
    n_chunks = u_ref.shape[1] // C
    row = lax.broadcasted_iota(jnp.int32, (C, C), 0)
    col = lax.broadcasted_iota(jnp.int32, (C, C), 1)
    for h in range(N_HEADS_SGU):
        cols = slice(h * HEAD_DIM, (h + 1) * HEAD_DIM)
        v = v_ref[0, :, cols]
        mu = jnp.mean(v, axis=-1, keepdims=True)
        vc = v - mu
        var = jnp.mean(vc * vc, axis=-1, keepdims=True)
        vn = vc * lax.rsqrt(var + LN_EPS) * lg_ref[h:h + 1, :] + lb_ref[h:h + 1, :]
        vn = vn.astype(BF16)
        ws = jnp.where(row >= col, w_ref[h], 0.0).astype(BF16)
        rhs = jnp.concatenate([vn[n * C:(n + 1) * C, :] for n in range(n_chunks)], axis=1)
        mixed = jnp.dot(ws, rhs, preferred_element_type=F32)
        for n in range(n_chunks):
            gate = mixed[:, n * HEAD_DIM:(n + 1) * HEAD_DIM] + b_ref[h]
            o_ref[0, n * C:(n + 1) * C, cols] = u_ref[0, n * C:(n + 1) * C, cols] * gate


def _sgu(uv, sgu_w, sgu_b_lanes, ln_g, ln_b):
    B, S, _ = uv.shape
    ts = SGU_TS
    Hs, C = N_HEADS_SGU, SGU_CHUNK
    return pl.pallas_call(
        _sgu_kernel,
        out_shape=jax.ShapeDtypeStruct((B, S, D_SGU), F32),
        grid=(B, S // ts),
        in_specs=[pl.BlockSpec((1, ts, D_SGU), lambda b, i: (b, i, 0)),
                  pl.BlockSpec((1, ts, D_SGU), lambda b, i: (b, i, 1)),
                  pl.BlockSpec((Hs, C, C), lambda b, i: (0, 0, 0)),
                  pl.BlockSpec((Hs, C, HEAD_DIM), lambda b, i: (0, 0, 0)),
                  pl.BlockSpec((Hs, HEAD_DIM), lambda b, i: (0, 0)),
                  pl.BlockSpec((Hs, HEAD_DIM), lambda b, i: (0, 0))],
        out_specs=pl.BlockSpec((1, ts, D_SGU), lambda b, i: (b, i, 0)),
        compiler_params=_params(("arbitrary", "arbitrary")),
        name="sgu",
    )(uv, uv, sgu_w, sgu_b_lanes, ln_g, ln_b)


OUTPROJ_TM = 512


def _outproj_kernel(a_ref, s_ref, x_ref, w_ref, ga_ref, gs_ref, gt_ref, sc_ref, sh_ref, g2_ref,
                    h_ref, hn_ref, hnT_ref):
    ra = _rms(a_ref[0], RMS_EPS) * ga_ref[...]
    rs = _rms(s_ref[0], RMS_EPS) * gs_ref[...]
    mixed = jnp.concatenate([ra, rs], axis=1).astype(BF16)
    y = jnp.dot(mixed, w_ref[...], preferred_element_type=F32)
    h = x_ref[0] + gt_ref[0] * y
    h_ref[0] = h
    hn = _rms(h, RMS_EPS) * g2_ref[...]
    hn = hn * (1.0 + sc_ref[0]) + sh_ref[0]
    hn_ref[0] = hn.astype(BF16)
    hnT_ref[...] = hn.T.astype(BF16)


def _outproj(attn, sgu, x, w_out, g_attn, g_sgu, gt1, sc2, sh2, g2):
    B, S, D = x.shape
    tm = OUTPROJ_TM
    nt = S // tm
    mod = lambda: pl.BlockSpec((1, 1, D), lambda b, i: (b, 0, 0))
    return pl.pallas_call(
        _outproj_kernel,
        out_shape=(jax.ShapeDtypeStruct((B, S, D), F32),
                   jax.ShapeDtypeStruct((B, S, D), BF16),
                   jax.ShapeDtypeStruct((D, B * S), BF16)),
        grid=(B, nt),
        in_specs=[pl.BlockSpec((1, tm, D_ATTN), lambda b, i: (b, i, 0)),
                  pl.BlockSpec((1, tm, D_SGU), lambda b, i: (b, i, 0)),
                  pl.BlockSpec((1, tm, D), lambda b, i: (b, i, 0)),
                  pl.BlockSpec((D, D), lambda b, i: (0, 0)),
                  pl.BlockSpec((1, D_ATTN), lambda b, i: (0, 0)),
                  pl.BlockSpec((1, D_SGU), lambda b, i: (0, 0)),
                  mod(), mod(), mod(),
                  pl.BlockSpec((1, D), lambda b, i: (0, 0))],
        out_specs=(pl.BlockSpec((1, tm, D), lambda b, i: (b, i, 0)),
                   pl.BlockSpec((1, tm, D), lambda b, i: (b, i, 0)),
                   pl.BlockSpec((D, tm), lambda b, i: (0, b * nt + i))),
        compiler_params=_params(("arbitrary", "arbitrary")),
        name="outproj",
    )(attn, sgu, x, w_out, g_attn, g_sgu, gt1, sc2, sh2, g2)


PEERK_TM = 256
TOP_ROWS = 24
N_TOP = PEER_TOPK + 1
NO_RANK = 64.0


def _extract_top(s, count, with_rank=False):
    tops = []
    rank = jnp.full(s.shape, NO_RANK, F32) if with_rank else None
    for k in range(count):
        m = jnp.max(s, axis=0, keepdims=True)
        tops.append(m)
        hit = s == m
        if with_rank:
            rank = jnp.where(hit, float(k + 1), rank)
        s = jnp.where(hit, NEG_INF, s)
    return (tops, rank) if with_rank else tops


def _peerk_kernel(hn_ref, wq_ref, sk_ref, cnt_ref, e1_ref, rk2_ref, e2_ref, q_s, s1_s, top_s):
    tm = hn_ref.shape[0]
    q_s[...] = jnp.dot(hn_ref[...], wq_ref[...], preferred_element_type=F32).astype(BF16)
    pad = jnp.full((TOP_ROWS - N_TOP, tm), NEG_INF, F32)

    def scores(hp):
        col0 = pl.multiple_of(hp * PEER_NKEYS, PEER_NKEYS)
        qh = q_s[:, pl.ds(col0, PEER_NKEYS)]
        return lax.dot_general(sk_ref[hp], qh, (((1,), (1,)), ((), ())), preferred_element_type=F32)

    def score_body(h, carry):
        s1 = scores(2 * h)
        s1_s[h] = s1
        top_s[2 * h] = jnp.concatenate(_extract_top(s1, N_TOP) + [pad], axis=0)
        s2 = scores(2 * h + 1)
        tops2, rank2 = _extract_top(s2, N_TOP, with_rank=True)
        top_s[2 * h + 1] = jnp.concatenate(tops2 + [pad], axis=0)
        rk2_ref[h] = rank2.astype(BF16)
        e2_ref[h] = jnp.exp(s2 - tops2[0]).astype(BF16)
        return carry

    lax.fori_loop(0, PEER_HEADS, score_body, 0, unroll=2)

    def head_body(h, carry):
        a = top_s[2 * h]
        b = top_s[2 * h + 1]
        groups = [a[0:1] + b]
        groups += [a[i:i + 1] + b[0:SUBLANES] for i in range(1, SUBLANES)]
        groups += [a[SUBLANES:TOP_ROWS] + b[0:1]]
        best = _extract_top(jnp.concatenate(groups, axis=0), N_TOP)
        z = jnp.zeros_like(best[0])
        for v in best[:PEER_TOPK]:
            z = z + jnp.exp(v - best[0])
        tau = 0.5 * (best[PEER_TOPK - 1] + best[PEER_TOPK])
        s1 = s1_s[h]
        thr = tau - s1
        cnt = jnp.zeros_like(thr)
        for k in range(PEER_TOPK):
            cnt = cnt + jnp.where(b[k:k + 1] > thr, 1.0, 0.0)
        cnt_ref[h] = cnt
        e1_ref[h] = jnp.exp(s1 - a[0:1]) / z
        return carry

    lax.fori_loop(0, PEER_HEADS, head_body, 0, unroll=4)


def _peer_keys(hn2, w_q, sub_keys):
    T, D = hn2.shape
    tm = PEERK_TM
    H, K = PEER_HEADS, PEER_NKEYS
    out = jax.ShapeDtypeStruct((H, K, T), F32)
    out16 = jax.ShapeDtypeStruct((H, K, T), BF16)
    ospec = lambda: pl.BlockSpec((H, K, tm), lambda i: (0, 0, i))
    return pl.pallas_call(
        _peerk_kernel,
        out_shape=(out, out, out16, out16),
        grid=(T // tm,),
        in_specs=[pl.BlockSpec((tm, D), lambda i: (i, 0)),
                  pl.BlockSpec((D, 2 * H * K), lambda i: (0, 0)),
                  pl.BlockSpec((2 * H, K, K), lambda i: (0, 0, 0))],
        out_specs=(ospec(), ospec(), ospec(), ospec()),
        scratch_shapes=[pltpu.VMEM((tm, 2 * H * K), BF16),
                        pltpu.VMEM((H, K, tm), F32),
                        pltpu.VMEM((2 * H, TOP_ROWS, tm), F32)],
        compiler_params=_params(("arbitrary",)),
        name="peer_keys",
    )(hn2, w_q, sub_keys)


PEER_TT = 512
PEER_TE = 1024


def _peer_kernel(hT_ref, u_ref, vt_ref, cnt_ref, e1_ref, rk2_ref, e2_ref, o_ref, p0, p1, a0, a1,
                 *, chunks_per_tile):
    g = pl.program_id(0)

    @pl.when(g == 0)
    def _():
        p1[...] = jnp.zeros(p1.shape, p1.dtype)

    @pl.when(jnp.logical_or(g == 0, (g + chunks_per_tile - 1) % chunks_per_tile == 0))
    def _():
        o_ref[...] = jnp.zeros(o_ref.shape, o_ref.dtype)

    K = PEER_NKEYS
    n_c = PEER_TE // K
    d_rows = o_ref.shape[0] // n_c

    def pre_act(c, a_w):
        rows = pl.ds(pl.multiple_of(c * K, K), K)
        a_w[...] = jnp.dot(u_ref[rows, :], hT_ref[...], preferred_element_type=F32)

    def piece(c, a_r, a_w, p_w, p_r, with_next):
        zero = jnp.zeros((), BF16)
        gate = None
        for h in range(PEER_HEADS):
            cnt = cnt_ref[h, pl.ds(c, 1), :].astype(BF16)
            picked = jnp.where(rk2_ref[h] <= cnt, e2_ref[h], zero)
            term = e1_ref[h, pl.ds(c, 1), :].astype(BF16) * picked
            gate = term if gate is None else gate + term
        p_new = gate * _gelu(a_r[...]).astype(BF16)
        dr = pl.ds(pl.multiple_of(c * d_rows, d_rows), d_rows)
        o_new = o_ref[dr, :] + jnp.dot(vt_ref[dr, :], p_r[...], preferred_element_type=F32)
        if with_next:
            rows = pl.ds(pl.multiple_of((c + 1) * K, K), K)
            a_new = jnp.dot(u_ref[rows, :], hT_ref[...], preferred_element_type=F32)
        p_w[pl.ds(pl.multiple_of(c * K, K), K), :] = p_new
        o_ref[dr, :] = o_new
        if with_next:
            a_w[...] = a_new

    def step(p_w, p_r):
        pre_act(0, a0)

        def body(i, carry):
            piece(2 * i, a0, a1, p_w, p_r, True)
            piece(2 * i + 1, a1, a0, p_w, p_r, True)
            return carry

        lax.fori_loop(0, n_c // 2 - 1, body, 0)
        piece(n_c - 2, a0, a1, p_w, p_r, True)
        piece(n_c - 1, a1, a0, p_w, p_r, False)

    @pl.when(g % 2 == 0)
    def _():
        step(p0, p1)

    @pl.when(g % 2 == 1)
    def _():
        step(p1, p0)


def _peer_ffn(hn2T, u, vT, cnt, e1, rk2, e2):
    D, T = hn2T.shape
    E = u.shape[0]
    tt, te = PEER_TT, PEER_TE
    H, K = PEER_HEADS, PEER_NKEYS
    nj = E // te
    n_chunks = (T // tt) * nj
    act = lambda g: jnp.minimum(g, n_chunks - 1)
    val = lambda g: jnp.maximum(g - 1, 0)
    per_c = pl.BlockSpec((H, te // K, tt), lambda g: (0, act(g) % nj, act(g) // nj))
    per_t = pl.BlockSpec((H, K, tt), lambda g: (0, 0, act(g) // nj))
    return pl.pallas_call(
        functools.partial(_peer_kernel, chunks_per_tile=nj),
        out_shape=jax.ShapeDtypeStruct((D, T), F32),
        grid=(n_chunks + 1,),
        in_specs=[pl.BlockSpec((D, tt), lambda g: (0, act(g) // nj)),
                  pl.BlockSpec((te, D), lambda g: (act(g) % nj, 0)),
                  pl.BlockSpec((D, te), lambda g: (0, val(g) % nj)),
                  per_c, per_c, per_t, per_t],
        out_specs=pl.BlockSpec((D, tt), lambda g: (0, val(g) // nj)),
        scratch_shapes=[pltpu.VMEM((te, tt), BF16),
                        pltpu.VMEM((te, tt), BF16),
                        pltpu.VMEM((K, tt), F32),
                        pltpu.VMEM((K, tt), F32)],
        compiler_params=_params(("arbitrary",)),
        name="peer_ffn",
    )(hn2T, u, vT, cnt, e1, rk2, e2)


FINAL_TM = 512


def _final_kernel(h_ref, pT_ref, gt_ref, g_ref, o_ref):
    h = h_ref[0] + gt_ref[0] * pT_ref[...].T
    o_ref[0] = _rms(h, RMS_EPS) * g_ref[...]


def _final(h1, peerT, gt2, g_final):
    B, S, D = h1.shape
    tm = FINAL_TM
    nt = S // tm
    return pl.pallas_call(
        _final_kernel,
        out_shape=jax.ShapeDtypeStruct((B, S, D), F32),
        grid=(B, nt),
        in_specs=[pl.BlockSpec((1, tm, D), lambda b, i: (b, i, 0)),
                  pl.BlockSpec((D, tm), lambda b, i: (0, b * nt + i)),
                  pl.BlockSpec((1, 1, D), lambda b, i: (b, 0, 0)),
                  pl.BlockSpec((1, D), lambda b, i: (0, 0))],
        out_specs=pl.BlockSpec((1, tm, D), lambda b, i: (b, i, 0)),
        compiler_params=_params(("arbitrary", "arbitrary")),
        name="final",
    )(h1, peerT, gt2, g_final)


def kernel(x, c, positions, w_ada, b_ada, g_norm1, w_in, g_attn_out, g_sgu_out, sgu_w, sgu_b,
           sgu_ln_g, sgu_ln_b, w_out, g_norm2, peer_w_q, peer_sub_keys, peer_u, peer_v, g_final):
    B, S, D = x.shape
    assert w_ada.shape[0] == 1, "single-layer block"
    l = 0
    half = HEAD_DIM // 2
    inv = ROPE_THETA ** (-jnp.arange(half, dtype=F32) / half)
    inv2 = jnp.concatenate([inv, inv]).reshape(1, HEAD_DIM)
    posf = positions.astype(F32).reshape(B, S, 1)
    row = lambda g: g.reshape(1, -1)

    mod = _adaln(c, w_ada[l], b_ada[l]).reshape(B, N_MOD, 1, D)
    sh1, sc1, gt1, sh2, sc2, gt2 = [mod[:, i] for i in range(N_MOD)]

    qkv, uv = _inproj(x, posf, inv2, sc1, sh1, row(g_norm1[l]), w_in[l].astype(BF16))
    attn = _attention(qkv)
    sgu_b_lanes = jnp.broadcast_to(sgu_b[l][:, :, None], (N_HEADS_SGU, SGU_CHUNK, HEAD_DIM))
    sgu = _sgu(uv, sgu_w[l], sgu_b_lanes, sgu_ln_g[l], sgu_ln_b[l])
    h1, hn2, hn2T = _outproj(attn, sgu, x, w_out[l].astype(BF16), row(g_attn_out[l]),
                             row(g_sgu_out[l]), gt1, sc2, sh2, row(g_norm2[l]))

    sk = peer_sub_keys[l].astype(BF16).reshape(2 * PEER_HEADS, PEER_NKEYS, -1)
    cnt, e1, rk2, e2 = _peer_keys(hn2.reshape(B * S, D), peer_w_q[l].astype(BF16), sk)
    peerT = _peer_ffn(hn2T, peer_u[l].astype(BF16), peer_v[l].T.astype(BF16), cnt, e1, rk2, e2)
    return _final(h1, peerT, gt2, row(g_final))
```

```python
import functools
import math

import jax
import jax.numpy as jnp
from jax import lax
from jax.experimental import pallas as pl
from jax.experimental.pallas import tpu as pltpu

F32 = jnp.float32
BF16 = jnp.bfloat16

HEAD_DIM = 128
N_HEADS_ATTN = 12
N_HEADS_SGU = 4
D_ATTN = N_HEADS_ATTN * HEAD_DIM
D_SGU = N_HEADS_SGU * HEAD_DIM
D_QKV = 3 * D_ATTN
DILATION_PATTERNS = ((128, 1), (512, 4), (2048, 16))
ATTN_STEPS = 128
ATTN_UNROLL = 32
ROPE_THETA = 10000.0
SGU_CHUNK = 128
PEER_HEADS = 8
PEER_NKEYS = 128
PEER_TOPK = 16
N_MOD = 6
RMS_EPS = 1e-6
LN_EPS = 1e-5

LANES = 128
SUBLANES = 8
VMEM_LIMIT = 56 * 1024 * 1024
NEG_INF = float("-inf")


def _params(semantics):
    return pltpu.CompilerParams(dimension_semantics=semantics, vmem_limit_bytes=VMEM_LIMIT)


def _gelu(x):
    return 0.5 * x * (1.0 + lax.erf(x * (1.0 / math.sqrt(2.0))))


def _rms(x, eps):
    return x * lax.rsqrt(jnp.mean(x * x, axis=-1, keepdims=True) + eps)


def _adaln_kernel(c_ref, w_ref, b_ref, o_ref):
    c = c_ref[...]
    cond = c * jax.nn.sigmoid(c)
    o_ref[...] = jnp.dot(cond.astype(BF16), w_ref[...].astype(BF16),
                         preferred_element_type=F32) + b_ref[...]


def _adaln(c, w_ada, b_ada):
    B, D = c.shape
    N = w_ada.shape[1]
    tn = 1024
    rows = SUBLANES
    c_pad = jnp.pad(c, ((0, rows - B), (0, 0)))
    out = pl.pallas_call(
        _adaln_kernel,
        out_shape=jax.ShapeDtypeStruct((rows, N), F32),
        grid=(N // tn,),
        in_specs=[pl.BlockSpec((rows, D), lambda j: (0, 0)),
                  pl.BlockSpec((D, tn), lambda j: (0, j)),
                  pl.BlockSpec((1, tn), lambda j: (0, j))],
        out_specs=pl.BlockSpec((rows, tn), lambda j: (0, j)),
        compiler_params=_params(("arbitrary",)),
        name="adaln",
    )(c_pad, w_ada, b_ada.reshape(1, N))
    return out[:B]


INPROJ_TM = 1024
INPROJ_TN = 512
_QK_TILES = 2 * D_ATTN // INPROJ_TN
_Q_TILES = D_ATTN // INPROJ_TN
_QKV_TILES = D_QKV // INPROJ_TN


def _inproj_kernel(x_ref, pos_ref, inv_ref, sc_ref, sh_ref, g_ref, w_ref,
                   qkv_ref, uv_ref, hn_ref, cos_ref, sin_ref):
    j = pl.program_id(2)

    @pl.when(j == 0)
    def _():
        x = x_ref[0]
        hn = _rms(x, RMS_EPS) * g_ref[...]
        hn = hn * (1.0 + sc_ref[0]) + sh_ref[0]
        hn_ref[...] = hn.astype(BF16)
        ang = pos_ref[0] * inv_ref[...]
        lane = lax.broadcasted_iota(jnp.int32, ang.shape, 1)
        cos_ref[...] = jnp.cos(ang)
        sin_ref[...] = jnp.where(lane < HEAD_DIM // 2, -1.0, 1.0) * jnp.sin(ang)

    acc = jnp.dot(hn_ref[...], w_ref[...], preferred_element_type=F32)

    @pl.when(j < _QK_TILES)
    def _():
        scale = jnp.where(j < _Q_TILES, HEAD_DIM ** -0.5, 1.0).astype(F32)
        cos = cos_ref[...]
        sin = sin_ref[...]
        for h in range(INPROJ_TN // HEAD_DIM):
            t = acc[:, h * HEAD_DIM:(h + 1) * HEAD_DIM]
            r = t * cos + pltpu.roll(t, HEAD_DIM // 2, 1) * sin
            qkv_ref[0, :, h * HEAD_DIM:(h + 1) * HEAD_DIM] = (r * scale).astype(BF16)

    @pl.when(jnp.logical_and(j >= _QK_TILES, j < _QKV_TILES))
    def _():
        qkv_ref[0] = acc.astype(BF16)

    @pl.when(j >= _QKV_TILES)
    def _():
        uv_ref[0] = _gelu(acc)


def _inproj(x, posf, inv2, sc1, sh1, g1, w_in):
    B, S, D = x.shape
    N = w_in.shape[1]
    tm, tn = INPROJ_TM, INPROJ_TN
    last_qkv = _QKV_TILES - 1
    return pl.pallas_call(
        _inproj_kernel,
        out_shape=(jax.ShapeDtypeStruct((B, S, D_QKV), BF16),
                   jax.ShapeDtypeStruct((B, S, 2 * D_SGU), F32)),
        grid=(B, S // tm, N // tn),
        in_specs=[pl.BlockSpec((1, tm, D), lambda b, i, j: (b, i, 0)),
                  pl.BlockSpec((1, tm, 1), lambda b, i, j: (b, i, 0)),
                  pl.BlockSpec((1, LANES), lambda b, i, j: (0, 0)),
                  pl.BlockSpec((1, 1, D), lambda b, i, j: (b, 0, 0)),
                  pl.BlockSpec((1, 1, D), lambda b, i, j: (b, 0, 0)),
                  pl.BlockSpec((1, D), lambda b, i, j: (0, 0)),
                  pl.BlockSpec((D, tn), lambda b, i, j: (0, j))],
        out_specs=(pl.BlockSpec((1, tm, tn), lambda b, i, j: (b, i, jnp.minimum(j, last_qkv))),
                   pl.BlockSpec((1, tm, tn), lambda b, i, j: (b, i, jnp.maximum(j - _QKV_TILES, 0)))),
        scratch_shapes=[pltpu.VMEM((tm, D), BF16),
                        pltpu.VMEM((tm, LANES), F32),
                        pltpu.VMEM((tm, LANES), F32)],
        compiler_params=_params(("arbitrary", "arbitrary", "arbitrary")),
        name="inproj",
    )(x, posf, inv2, sc1, sh1, g1, w_in)


def _attn_kernel(q_ref, k_ref, v_ref, o_ref, nat, q4, k4, v4, qd, kd, vd, acc, m_s, l_s):
    S = q_ref.shape[1]
    steps = ATTN_STEPS
    n_blocks = S // steps
    (_, d_min), (_, d_mid), (_, d_max) = DILATION_PATTERNS
    assert d_min == 1 and d_max == d_mid * d_mid
    L_mid = S // d_mid
    for src_ref, grouped in ((q_ref, q4), (k_ref, k4), (v_ref, v4)):
        nat[...] = src_ref[0].astype(F32)
        for r in range(d_mid):
            grouped[r * L_mid:(r + 1) * L_mid, :] = nat[pl.ds(r, L_mid, stride=d_mid), :]
    kd[0:steps, :] = jnp.zeros((steps, HEAD_DIM), BF16)
    vd[0:steps, :] = jnp.zeros((steps, HEAD_DIM), BF16)

    row = lax.broadcasted_iota(jnp.int32, (steps, 2 * steps), 0)
    km = lax.broadcasted_iota(jnp.int32, (steps, 2 * steps), 1)
    band = jnp.logical_and(km >= row, km <= row + steps)
    run = steps // d_mid
    perm = (row % run) * d_mid + row // run
    band_grouped = jnp.logical_and(km >= perm, km <= perm + steps)

    def merge(rows, sl, ob, mb_b, lb_b, init):
        if init:
            acc[rows, :] = ob[sl]
            m_s[rows, :] = mb_b[sl]
            l_s[rows, :] = lb_b[sl]
        else:
            m_old = m_s[rows, :]
            m_new = jnp.maximum(m_old, mb_b[sl])
            a_old = jnp.exp(m_old - m_new)
            a_blk = jnp.exp(mb_b[sl] - m_new)
            acc[rows, :] = acc[rows, :] * a_old + ob[sl] * a_blk
            l_s[rows, :] = l_s[rows, :] * a_old + lb_b[sl] * a_blk
            m_s[rows, :] = m_new

    for pi, (window, d) in enumerate(reversed(DILATION_PATTERNS)):
        assert window // d == steps
        L = S // d
        nb = L // steps
        if d == 1:
            kd[steps:, :] = k_ref[0]
            vd[steps:, :] = v_ref[0]
        elif d == d_mid:
            for src, dst, off in ((q4, qd, 0), (k4, kd, steps), (v4, vd, steps)):
                dst[off:off + S, :] = src[...].astype(BF16)
        else:
            for r in range(d):
                start = (r % d_mid) * L_mid + r // d_mid
                for src, dst, off in ((q4, qd, 0), (k4, kd, steps), (v4, vd, steps)):
                    dst[off + r * L:off + (r + 1) * L, :] = src[pl.ds(start, L, stride=d_mid), :].astype(BF16)

        def body(g, carry, pi=pi, d=d, nb=nb):
            n = g % nb
            r = g // nb
            row0 = pl.multiple_of(g * steps, steps)
            if d == 1:
                run0 = pl.multiple_of(g * run, run)
                qb = jnp.concatenate([q4[pl.ds(c * L_mid + run0, run), :] for c in range(d_mid)],
                                     axis=0).astype(BF16)
            else:
                qb = qd[pl.ds(row0, steps), :]
            kc = kd[pl.ds(row0, 2 * steps), :]
            vc = vd[pl.ds(row0, 2 * steps), :]
            s = lax.dot_general(qb, kc, (((1,), (1,)), ((), ())), preferred_element_type=F32)
            first_key = jnp.where(n > 0, 0, steps)
            in_band = band_grouped if d == 1 else band
            s = jnp.where(jnp.logical_and(in_band, km >= first_key), s, NEG_INF)
            mb = jnp.max(s, axis=-1, keepdims=True)
            p = jnp.exp(s - mb)
            lb = jnp.sum(p, axis=-1, keepdims=True)
            ob = jnp.dot(p.astype(BF16), vc, preferred_element_type=F32)
            mb_b = jnp.broadcast_to(mb, (steps, HEAD_DIM))
            lb_b = jnp.broadcast_to(lb, (steps, HEAD_DIM))
            everything = slice(0, steps)
            if d == 1:
                for c in range(d_mid):
                    merge(pl.ds(c * L_mid + run0, run), slice(c * run, (c + 1) * run),
                          ob, mb_b, lb_b, pi == 0)
            elif d == d_mid:
                merge(pl.ds(row0, steps), everything, ob, mb_b, lb_b, pi == 0)
            else:
                start = (r % d_mid) * L_mid + r // d_mid + n * (steps * d_mid)
                merge(pl.ds(start, steps, stride=d_mid), everything, ob, mb_b, lb_b, pi == 0)
            return carry

        lax.fori_loop(0, n_blocks, body, 0, unroll=ATTN_UNROLL)

    for c in range(d_mid):
        grp = slice(c * L_mid, (c + 1) * L_mid)
        o_ref[0, pl.ds(c, L_mid, stride=d_mid), :] = acc[grp, :] / l_s[grp, :]


def _attention(qkv):
    B, S, _ = qkv.shape
    H = N_HEADS_ATTN
    blk = lambda off: pl.BlockSpec((1, S, HEAD_DIM), lambda b, h: (b, 0, off + h))
    return pl.pallas_call(
        _attn_kernel,
        out_shape=jax.ShapeDtypeStruct((B, S, D_ATTN), F32),
        grid=(B, H),
        in_specs=[blk(0), blk(H), blk(2 * H)],
        out_specs=pl.BlockSpec((1, S, HEAD_DIM), lambda b, h: (b, 0, h)),
        scratch_shapes=[pltpu.VMEM((S, HEAD_DIM), F32),
                        pltpu.VMEM((S, HEAD_DIM), F32),
                        pltpu.VMEM((S, HEAD_DIM), F32),
                        pltpu.VMEM((S, HEAD_DIM), F32),
                        pltpu.VMEM((S, HEAD_DIM), BF16),
                        pltpu.VMEM((S + ATTN_STEPS, HEAD_DIM), BF16),
                        pltpu.VMEM((S + ATTN_STEPS, HEAD_DIM), BF16),
                        pltpu.VMEM((S, HEAD_DIM), F32),
                        pltpu.VMEM((S, HEAD_DIM), F32),
                        pltpu.VMEM((S, HEAD_DIM), F32)],
        compiler_params=_params(("arbitrary", "arbitrary")),
        name="attn",
    )(qkv, qkv, qkv)


SGU_TS = 512


def _sgu_kernel(u_ref, v_ref, w_ref, b_ref, lg_ref, lb_ref, o_ref):
    C = SGU_CHUNK
    n_chunks = u_ref.shape[1] // C
    row = lax.broadcasted_iota(jnp.int32, (C, C), 0)
    col = lax.broadcasted_iota(jnp.int32, (C, C), 1)
    for h in range(N_HEADS_SGU):
        cols = slice(h * HEAD_DIM, (h + 1) * HEAD_DIM)
        v = v_ref[0, :, cols]
        mu = jnp.mean(v, axis=-1, keepdims=True)
        vc = v - mu
        var = jnp.mean(vc * vc, axis=-1, keepdims=True)
        vn = vc * lax.rsqrt(var + LN_EPS) * lg_ref[h:h + 1, :] + lb_ref[h:h + 1, :]
        vn = vn.astype(BF16)
        ws = jnp.where(row >= col, w_ref[h], 0.0).astype(BF16)
        rhs = jnp.concatenate([vn[n * C:(n + 1) * C, :] for n in range(n_chunks)], axis=1)
        mixed = jnp.dot(ws, rhs, preferred_element_type=F32)
        for n in range(n_chunks):
            gate = mixed[:, n * HEAD_DIM:(n + 1) * HEAD_DIM] + b_ref[h]
            o_ref[0, n * C:(n + 1) * C, cols] = u_ref[0, n * C:(n + 1) * C, cols] * gate


def _sgu(uv, sgu_w, sgu_b_lanes, ln_g, ln_b):
    B, S, _ = uv.shape
    ts = SGU_TS
    Hs, C = N_HEADS_SGU, SGU_CHUNK
    return pl.pallas_call(
        _sgu_kernel,
        out_shape=jax.ShapeDtypeStruct((B, S, D_SGU), F32),
        grid=(B, S // ts),
        in_specs=[pl.BlockSpec((1, ts, D_SGU), lambda b, i: (b, i, 0)),
                  pl.BlockSpec((1, ts, D_SGU), lambda b, i: (b, i, 1)),
                  pl.BlockSpec((Hs, C, C), lambda b, i: (0, 0, 0)),
                  pl.BlockSpec((Hs, C, HEAD_DIM), lambda b, i: (0, 0, 0)),
                  pl.BlockSpec((Hs, HEAD_DIM), lambda b, i: (0, 0)),
                  pl.BlockSpec((Hs, HEAD_DIM), lambda b, i: (0, 0))],
        out_specs=pl.BlockSpec((1, ts, D_SGU), lambda b, i: (b, i, 0)),
        compiler_params=_params(("arbitrary", "arbitrary")),
        name="sgu",
    )(uv, uv, sgu_w, sgu_b_lanes, ln_g, ln_b)


OUTPROJ_TM = 512


def _outproj_kernel(a_ref, s_ref, x_ref, w_ref, ga_ref, gs_ref, gt_ref, sc_ref, sh_ref, g2_ref,
                    h_ref, hn_ref, hnT_ref):
    ra = _rms(a_ref[0], RMS_EPS) * ga_ref[...]
    rs = _rms(s_ref[0], RMS_EPS) * gs_ref[...]
    mixed = jnp.concatenate([ra, rs], axis=1).astype(BF16)
    y = jnp.dot(mixed, w_ref[...], preferred_element_type=F32)
    h = x_ref[0] + gt_ref[0] * y
    h_ref[0] = h
    hn = _rms(h, RMS_EPS) * g2_ref[...]
    hn = hn * (1.0 + sc_ref[0]) + sh_ref[0]
    hn_ref[0] = hn.astype(BF16)
    hnT_ref[...] = hn.T.astype(BF16)


def _outproj(attn, sgu, x, w_out, g_attn, g_sgu, gt1, sc2, sh2, g2):
    B, S, D = x.shape
    tm = OUTPROJ_TM
    nt = S // tm
    mod = lambda: pl.BlockSpec((1, 1, D), lambda b, i: (b, 0, 0))
    return pl.pallas_call(
        _outproj_kernel,
        out_shape=(jax.ShapeDtypeStruct((B, S, D), F32),
                   jax.ShapeDtypeStruct((B, S, D), BF16),
                   jax.ShapeDtypeStruct((D, B * S), BF16)),
        grid=(B, nt),
        in_specs=[pl.BlockSpec((1, tm, D_ATTN), lambda b, i: (b, i, 0)),
                  pl.BlockSpec((1, tm, D_SGU), lambda b, i: (b, i, 0)),
                  pl.BlockSpec((1, tm, D), lambda b, i: (b, i, 0)),
                  pl.BlockSpec((D, D), lambda b, i: (0, 0)),
                  pl.BlockSpec((1, D_ATTN), lambda b, i: (0, 0)),
                  pl.BlockSpec((1, D_SGU), lambda b, i: (0, 0)),
                  mod(), mod(), mod(),
                  pl.BlockSpec((1, D), lambda b, i: (0, 0))],
        out_specs=(pl.BlockSpec((1, tm, D), lambda b, i: (b, i, 0)),
                   pl.BlockSpec((1, tm, D), lambda b, i: (b, i, 0)),
                   pl.BlockSpec((D, tm), lambda b, i: (0, b * nt + i))),
        compiler_params=_params(("arbitrary", "arbitrary")),
        name="outproj",
    )(attn, sgu, x, w_out, g_attn, g_sgu, gt1, sc2, sh2, g2)


PEERK_TM = 256
TOP_ROWS = 24
N_TOP = PEER_TOPK + 1
NO_RANK = 64.0


def _extract_top(s, count, with_rank=False):
    tops = []
    rank = jnp.full(s.shape, NO_RANK, F32) if with_rank else None
    for k in range(count):
        m = jnp.max(s, axis=0, keepdims=True)
        tops.append(m)
        hit = s == m
        if with_rank:
            rank = jnp.where(hit, float(k + 1), rank)
        s = jnp.where(hit, NEG_INF, s)
    return (tops, rank) if with_rank else tops


def _count_above(b, thr):
    rows = [b[k:k + 1] for k in range(PEER_TOPK)]
    bits = []
    span = PEER_TOPK // 2
    while span >= 1:
        level = [rows[p * 2 * span + span - 1] for p in range(2 ** len(bits))]
        for m in reversed(bits):
            level = [jnp.where(m, level[2 * i + 1], level[2 * i]) for i in range(len(level) // 2)]
        bits.append(level[0] > thr)
        span //= 2
    cnt = jnp.where(rows[PEER_TOPK - 1] > thr, 1.0, 0.0)
    for i, m in enumerate(bits):
        cnt = cnt + jnp.where(m, float(PEER_TOPK >> (i + 1)), 0.0)
    return cnt


def _peerk_kernel(hn_ref, wq_ref, sk_ref, cnt_ref, e1_ref, rk2_ref, e2_ref, q_s, s1_s, top_s):
    tm = hn_ref.shape[0]
    q_s[...] = jnp.dot(hn_ref[...], wq_ref[...], preferred_element_type=F32).astype(BF16)
    pad = jnp.full((TOP_ROWS - N_TOP, tm), NEG_INF, F32)

    def scores(hp):
        col0 = pl.multiple_of(hp * PEER_NKEYS, PEER_NKEYS)
        qh = q_s[:, pl.ds(col0, PEER_NKEYS)]
        return lax.dot_general(sk_ref[hp], qh, (((1,), (1,)), ((), ())), preferred_element_type=F32)

    def score_body(h, carry):
        s1 = scores(2 * h)
        s1_s[h] = s1
        top_s[2 * h] = jnp.concatenate(_extract_top(s1, N_TOP) + [pad], axis=0)
        s2 = scores(2 * h + 1)
        tops2, rank2 = _extract_top(s2, N_TOP, with_rank=True)
        top_s[2 * h + 1] = jnp.concatenate(tops2 + [pad], axis=0)
        rk2_ref[h] = rank2.astype(BF16)
        e2_ref[h] = jnp.exp(s2 - tops2[0]).astype(BF16)
        return carry

    lax.fori_loop(0, PEER_HEADS, score_body, 0, unroll=4)

    def head_body(h, carry):
        a = top_s[2 * h]
        b = top_s[2 * h + 1]
        groups = [a[0:1] + b]
        groups += [a[i:i + 1] + b[0:SUBLANES] for i in range(1, SUBLANES)]
        groups += [a[SUBLANES:TOP_ROWS] + b[0:1]]
        best = _extract_top(jnp.concatenate(groups, axis=0), N_TOP)
        z = jnp.zeros_like(best[0])
        for v in best[:PEER_TOPK]:
            z = z + jnp.exp(v - best[0])
        tau = 0.5 * (best[PEER_TOPK - 1] + best[PEER_TOPK])
        s1 = s1_s[h]
        thr = tau - s1
        cnt_ref[h] = _count_above(b, thr)
        e1_ref[h] = jnp.exp(s1 - a[0:1]) / z
        return carry

    lax.fori_loop(0, PEER_HEADS, head_body, 0, unroll=4)


def _peer_keys(hn2, w_q, sub_keys):
    T, D = hn2.shape
    tm = PEERK_TM
    H, K = PEER_HEADS, PEER_NKEYS
    out = jax.ShapeDtypeStruct((H, K, T), F32)
    out16 = jax.ShapeDtypeStruct((H, K, T), BF16)
    ospec = lambda: pl.BlockSpec((H, K, tm), lambda i: (0, 0, i))
    return pl.pallas_call(
        _peerk_kernel,
        out_shape=(out, out, out16, out16),
        grid=(T // tm,),
        in_specs=[pl.BlockSpec((tm, D), lambda i: (i, 0)),
                  pl.BlockSpec((D, 2 * H * K), lambda i: (0, 0)),
                  pl.BlockSpec((2 * H, K, K), lambda i: (0, 0, 0))],
        out_specs=(ospec(), ospec(), ospec(), ospec()),
        scratch_shapes=[pltpu.VMEM((tm, 2 * H * K), BF16),
                        pltpu.VMEM((H, K, tm), F32),
                        pltpu.VMEM((2 * H, TOP_ROWS, tm), F32)],
        compiler_params=_params(("arbitrary",)),
        name="peer_keys",
    )(hn2, w_q, sub_keys)


PEER_TT = 512
PEER_TE = 1024


def _peer_kernel(hT_ref, u_ref, vt_ref, cnt_ref, e1_ref, rk2_ref, e2_ref, o_ref, p0, p1, a0, a1,
                 *, chunks_per_tile):
    g = pl.program_id(0)

    @pl.when(g == 0)
    def _():
        p1[...] = jnp.zeros(p1.shape, p1.dtype)

    @pl.when(jnp.logical_or(g == 0, (g + chunks_per_tile - 1) % chunks_per_tile == 0))
    def _():
        o_ref[...] = jnp.zeros(o_ref.shape, o_ref.dtype)

    K = PEER_NKEYS
    n_c = PEER_TE // K
    d_rows = o_ref.shape[0] // n_c

    def pre_act(c, a_w):
        rows = pl.ds(pl.multiple_of(c * K, K), K)
        a_w[...] = jnp.dot(u_ref[rows, :], hT_ref[...], preferred_element_type=F32)

    def piece(c, a_r, a_w, p_w, p_r, with_next):
        zero = jnp.zeros((), BF16)
        gate = None
        for h in range(PEER_HEADS):
            cnt = cnt_ref[h, pl.ds(c, 1), :].astype(BF16)
            picked = jnp.where(rk2_ref[h] <= cnt, e2_ref[h], zero)
            term = e1_ref[h, pl.ds(c, 1), :].astype(BF16) * picked
            gate = term if gate is None else gate + term
        p_new = gate * _gelu(a_r[...]).astype(BF16)
        dr = pl.ds(pl.multiple_of(c * d_rows, d_rows), d_rows)
        o_new = o_ref[dr, :] + jnp.dot(vt_ref[dr, :], p_r[...], preferred_element_type=F32)
        if with_next:
            nxt = jnp.minimum(c + 1, n_c - 1)
            rows = pl.ds(pl.multiple_of(nxt * K, K), K)
            a_new = jnp.dot(u_ref[rows, :], hT_ref[...], preferred_element_type=F32)
        p_w[pl.ds(pl.multiple_of(c * K, K), K), :] = p_new
        o_ref[dr, :] = o_new
        if with_next:
            a_w[...] = a_new

    def step(p_w, p_r):
        pre_act(0, a0)

        def body(i, carry):
            piece(2 * i, a0, a1, p_w, p_r, True)
            piece(2 * i + 1, a1, a0, p_w, p_r, True)
            return carry

        lax.fori_loop(0, n_c // 2, body, 0)

    @pl.when(g % 2 == 0)
    def _():
        step(p0, p1)

    @pl.when(g % 2 == 1)
    def _():
        step(p1, p0)


def _peer_ffn(hn2T, u, vT, cnt, e1, rk2, e2):
    D, T = hn2T.shape
    E = u.shape[0]
    tt, te = PEER_TT, PEER_TE
    H, K = PEER_HEADS, PEER_NKEYS
    nj = E // te
    n_chunks = (T // tt) * nj
    act = lambda g: jnp.minimum(g, n_chunks - 1)
    val = lambda g: jnp.maximum(g - 1, 0)
    per_c = pl.BlockSpec((H, te // K, tt), lambda g: (0, act(g) % nj, act(g) // nj))
    per_t = pl.BlockSpec((H, K, tt), lambda g: (0, 0, act(g) // nj))
    return pl.pallas_call(
        functools.partial(_peer_kernel, chunks_per_tile=nj),
        out_shape=jax.ShapeDtypeStruct((D, T), F32),
        grid=(n_chunks + 1,),
        in_specs=[pl.BlockSpec((D, tt), lambda g: (0, act(g) // nj)),
                  pl.BlockSpec((te, D), lambda g: (act(g) % nj, 0)),
                  pl.BlockSpec((D, te), lambda g: (0, val(g) % nj)),
                  per_c, per_c, per_t, per_t],
        out_specs=pl.BlockSpec((D, tt), lambda g: (0, val(g) // nj)),
        scratch_shapes=[pltpu.VMEM((te, tt), BF16),
                        pltpu.VMEM((te, tt), BF16),
                        pltpu.VMEM((K, tt), F32),
                        pltpu.VMEM((K, tt), F32)],
        compiler_params=_params(("arbitrary",)),
        name="peer_ffn",
    )(hn2T, u, vT, cnt, e1, rk2, e2)


FINAL_TM = 512


def _final_kernel(h_ref, pT_ref, gt_ref, g_ref, o_ref):
    h = h_ref[0] + gt_ref[0] * pT_ref[...].T
    o_ref[0] = _rms(h, RMS_EPS) * g_ref[...]


def _final(h1, peerT, gt2, g_final):
    B, S, D = h1.shape
    tm = FINAL_TM
    nt = S // tm
    return pl.pallas_call(
        _final_kernel,
        out_shape=jax.ShapeDtypeStruct((B, S, D), F32),
        grid=(B, nt),
        in_specs=[pl.BlockSpec((1, tm, D), lambda b, i: (b, i, 0)),
                  pl.BlockSpec((D, tm), lambda b, i: (0, b * nt + i)),
                  pl.BlockSpec((1, 1, D), lambda b, i: (b, 0, 0)),
                  pl.BlockSpec((1, D), lambda b, i: (0, 0))],
        out_specs=pl.BlockSpec((1, tm, D), lambda b, i: (b, i, 0)),
        compiler_params=_params(("arbitrary", "arbitrary")),
        name="final",
    )(h1, peerT, gt2, g_final)


def kernel(x, c, positions, w_ada, b_ada, g_norm1, w_in, g_attn_out, g_sgu_out, sgu_w, sgu_b,
           sgu_ln_g, sgu_ln_b, w_out, g_norm2, peer_w_q, peer_sub_keys, peer_u, peer_v, g_final):
    B, S, D = x.shape
    assert w_ada.shape[0] == 1, "single-layer block"
    l = 0
    half = HEAD_DIM // 2
    inv = ROPE_THETA ** (-jnp.arange(half, dtype=F32) / half)
    inv2 = jnp.concatenate([inv, inv]).reshape(1, HEAD_DIM)
    posf = positions.astype(F32).reshape(B, S, 1)
    row = lambda g: g.reshape(1, -1)

    mod = _adaln(c, w_ada[l], b_ada[l]).reshape(B, N_MOD, 1, D)
    sh1, sc1, gt1, sh2, sc2, gt2 = [mod[:, i] for i in range(N_MOD)]

    qkv, uv = _inproj(x, posf, inv2, sc1, sh1, row(g_norm1[l]), w_in[l].astype(BF16))
    attn = _attention(qkv)
    sgu_b_lanes = jnp.broadcast_to(sgu_b[l][:, :, None], (N_HEADS_SGU, SGU_CHUNK, HEAD_DIM))
    sgu = _sgu(uv, sgu_w[l], sgu_b_lanes, sgu_ln_g[l], sgu_ln_b[l])
    h1, hn2, hn2T = _outproj(attn, sgu, x, w_out[l].astype(BF16), row(g_attn_out[l]),
                             row(g_sgu_out[l]), gt1, sc2, sh2, row(g_norm2[l]))

    sk = peer_sub_keys[l].astype(BF16).reshape(2 * PEER_HEADS, PEER_NKEYS, -1)
    cnt, e1, rk2, e2 = _peer_keys(hn2.reshape(B * S, D), peer_w_q[l].astype(BF16), sk)
    peerT = _peer_ffn(hn2T, peer_u[l].astype(BF16), peer_v[l].T.astype(BF16), cnt, e1, rk2, e2)
    return _final(h1, peerT, gt2, row(g_final))
```

```python
import functools
import math

import jax
import jax.numpy as jnp
from jax import lax
from jax.experimental import pallas as pl
from jax.experimental.pallas import tpu as pltpu

F32 = jnp.float32
BF16 = jnp.bfloat16

HEAD_DIM = 128
N_HEADS_ATTN = 12
N_HEADS_SGU = 4
D_ATTN = N_HEADS_ATTN * HEAD_DIM
D_SGU = N_HEADS_SGU * HEAD_DIM
D_QKV = 3 * D_ATTN
DILATION_PATTERNS = ((128, 1), (512, 4), (2048, 16))
ATTN_STEPS = 128
ATTN_UNROLL = 32
ROPE_THETA = 10000.0
SGU_CHUNK = 128
PEER_HEADS = 8
PEER_NKEYS = 128
PEER_TOPK = 16
N_MOD = 6
RMS_EPS = 1e-6
LN_EPS = 1e-5

LANES = 128
SUBLANES = 8
VMEM_LIMIT = 56 * 1024 * 1024
NEG_INF = float("-inf")


def _params(semantics):
    return pltpu.CompilerParams(dimension_semantics=semantics, vmem_limit_bytes=VMEM_LIMIT)


def _gelu(x):
    return 0.5 * x * (1.0 + lax.erf(x * (1.0 / math.sqrt(2.0))))


def _rms(x, eps):
    return x * lax.rsqrt(jnp.mean(x * x, axis=-1, keepdims=True) + eps)


def _adaln_kernel(c_ref, w_ref, b_ref, o_ref):
    c = c_ref[...]
    cond = c * jax.nn.sigmoid(c)
    o_ref[...] = jnp.dot(cond.astype(BF16), w_ref[...].astype(BF16),
                         preferred_element_type=F32) + b_ref[...]


def _adaln(c, w_ada, b_ada):
    B, D = c.shape
    N = w_ada.shape[1]
    tn = 1024
    rows = SUBLANES
    c_pad = jnp.pad(c, ((0, rows - B), (0, 0)))
    out = pl.pallas_call(
        _adaln_kernel,
        out_shape=jax.ShapeDtypeStruct((rows, N), F32),
        grid=(N // tn,),
        in_specs=[pl.BlockSpec((rows, D), lambda j: (0, 0)),
                  pl.BlockSpec((D, tn), lambda j: (0, j)),
                  pl.BlockSpec((1, tn), lambda j: (0, j))],
        out_specs=pl.BlockSpec((rows, tn), lambda j: (0, j)),
        compiler_params=_params(("arbitrary",)),
        name="adaln",
    )(c_pad, w_ada, b_ada.reshape(1, N))
    return out[:B]


INPROJ_TM = 1024
INPROJ_TN = 512
_QK_TILES = 2 * D_ATTN // INPROJ_TN
_Q_TILES = D_ATTN // INPROJ_TN
_QKV_TILES = D_QKV // INPROJ_TN


def _inproj_kernel(x_ref, pos_ref, inv_ref, sc_ref, sh_ref, g_ref, w_ref,
                   qkv_ref, uv_ref, hn_ref, cos_ref, sin_ref):
    j = pl.program_id(2)

    @pl.when(j == 0)
    def _():
        x = x_ref[0]
        hn = _rms(x, RMS_EPS) * g_ref[...]
        hn = hn * (1.0 + sc_ref[0]) + sh_ref[0]
        hn_ref[...] = hn.astype(BF16)
        ang = pos_ref[0] * inv_ref[...]
        lane = lax.broadcasted_iota(jnp.int32, ang.shape, 1)
        cos_ref[...] = jnp.cos(ang)
        sin_ref[...] = jnp.where(lane < HEAD_DIM // 2, -1.0, 1.0) * jnp.sin(ang)

    def project():
        return jnp.dot(hn_ref[...], w_ref[...], preferred_element_type=F32)

    @pl.when(j < _QK_TILES)
    def _():
        scale = jnp.where(j < _Q_TILES, HEAD_DIM ** -0.5, 1.0).astype(F32)
        cos = cos_ref[...] * scale
        sin = sin_ref[...] * scale
        acc = project()
        for h in range(INPROJ_TN // HEAD_DIM):
            cols = slice(h * HEAD_DIM, (h + 1) * HEAD_DIM)
            t = acc[:, cols]
            r = t * cos + pltpu.roll(t, HEAD_DIM // 2, 1) * sin
            qkv_ref[0, :, cols] = r.astype(BF16)

    @pl.when(jnp.logical_and(j >= _QK_TILES, j < _QKV_TILES))
    def _():
        qkv_ref[0] = project().astype(BF16)

    @pl.when(j >= _QKV_TILES)
    def _():
        uv_ref[0] = _gelu(project())


def _inproj(x, posf, inv2, sc1, sh1, g1, w_in):
    B, S, D = x.shape
    N = w_in.shape[1]
    tm, tn = INPROJ_TM, INPROJ_TN
    last_qkv = _QKV_TILES - 1
    return pl.pallas_call(
        _inproj_kernel,
        out_shape=(jax.ShapeDtypeStruct((B, S, D_QKV), BF16),
                   jax.ShapeDtypeStruct((B, S, 2 * D_SGU), F32)),
        grid=(B, S // tm, N // tn),
        in_specs=[pl.BlockSpec((1, tm, D), lambda b, i, j: (b, i, 0)),
                  pl.BlockSpec((1, tm, 1), lambda b, i, j: (b, i, 0)),
                  pl.BlockSpec((1, LANES), lambda b, i, j: (0, 0)),
                  pl.BlockSpec((1, 1, D), lambda b, i, j: (b, 0, 0)),
                  pl.BlockSpec((1, 1, D), lambda b, i, j: (b, 0, 0)),
                  pl.BlockSpec((1, D), lambda b, i, j: (0, 0)),
                  pl.BlockSpec((D, tn), lambda b, i, j: (0, j))],
        out_specs=(pl.BlockSpec((1, tm, tn), lambda b, i, j: (b, i, jnp.minimum(j, last_qkv))),
                   pl.BlockSpec((1, tm, tn), lambda b, i, j: (b, i, jnp.maximum(j - _QKV_TILES, 0)))),
        scratch_shapes=[pltpu.VMEM((tm, D), BF16),
                        pltpu.VMEM((tm, LANES), F32),
                        pltpu.VMEM((tm, LANES), F32)],
        compiler_params=_params(("arbitrary", "arbitrary", "arbitrary")),
        name="inproj",
    )(x, posf, inv2, sc1, sh1, g1, w_in)


def _attn_kernel(q_ref, k_ref, v_ref, o_ref, nat, q4, k4, v4, qd, kd, vd, acc, m_s, l_s):
    S = q_ref.shape[1]
    steps = ATTN_STEPS
    n_blocks = S // steps
    (_, d_min), (_, d_mid), (_, d_max) = DILATION_PATTERNS
    assert d_min == 1 and d_max == d_mid * d_mid
    L_mid = S // d_mid
    for src_ref, grouped in ((q_ref, q4), (k_ref, k4), (v_ref, v4)):
        nat[...] = src_ref[0].astype(F32)
        for r in range(d_mid):
            grouped[r * L_mid:(r + 1) * L_mid, :] = nat[pl.ds(r, L_mid, stride=d_mid), :]
    kd[0:steps, :] = jnp.zeros((steps, HEAD_DIM), BF16)
    vd[0:steps, :] = jnp.zeros((steps, HEAD_DIM), BF16)

    row = lax.broadcasted_iota(jnp.int32, (steps, 2 * steps), 0)
    km = lax.broadcasted_iota(jnp.int32, (steps, 2 * steps), 1)
    band = jnp.logical_and(km >= row, km <= row + steps)
    run = steps // d_mid
    perm = (row % run) * d_mid + row // run
    band_grouped = jnp.logical_and(km >= perm, km <= perm + steps)

    def merge(rows, sl, ob, mb_b, lb_b, init):
        if init:
            acc[rows, :] = ob[sl]
            m_s[rows, :] = mb_b[sl]
            l_s[rows, :] = lb_b[sl]
        else:
            m_old = m_s[rows, :]
            m_new = jnp.maximum(m_old, mb_b[sl])
            a_old = jnp.exp(m_old - m_new)
            a_blk = jnp.exp(mb_b[sl] - m_new)
            acc[rows, :] = acc[rows, :] * a_old + ob[sl] * a_blk
            l_s[rows, :] = l_s[rows, :] * a_old + lb_b[sl] * a_blk
            m_s[rows, :] = m_new

    for pi, (window, d) in enumerate(reversed(DILATION_PATTERNS)):
        assert window // d == steps
        L = S // d
        nb = L // steps
        if d == 1:
            kd[steps:, :] = k_ref[0]
            vd[steps:, :] = v_ref[0]
        elif d == d_mid:
            for src, dst, off in ((q4, qd, 0), (k4, kd, steps), (v4, vd, steps)):
                dst[off:off + S, :] = src[...].astype(BF16)
        else:
            for r in range(d):
                start = (r % d_mid) * L_mid + r // d_mid
                for src, dst, off in ((q4, qd, 0), (k4, kd, steps), (v4, vd, steps)):
                    dst[off + r * L:off + (r + 1) * L, :] = src[pl.ds(start, L, stride=d_mid), :].astype(BF16)

        def body(g, carry, pi=pi, d=d, nb=nb):
            n = g % nb
            r = g // nb
            row0 = pl.multiple_of(g * steps, steps)
            if d == 1:
                run0 = pl.multiple_of(g * run, run)
                qb = jnp.concatenate([q4[pl.ds(c * L_mid + run0, run), :] for c in range(d_mid)],
                                     axis=0).astype(BF16)
            else:
                qb = qd[pl.ds(row0, steps), :]
            kc = kd[pl.ds(row0, 2 * steps), :]
            vc = vd[pl.ds(row0, 2 * steps), :]
            s = lax.dot_general(qb, kc, (((1,), (1,)), ((), ())), preferred_element_type=F32)
            first_key = jnp.where(n > 0, 0, steps)
            in_band = band_grouped if d == 1 else band
            s = jnp.where(jnp.logical_and(in_band, km >= first_key), s, NEG_INF)
            mb = jnp.max(s, axis=-1, keepdims=True)
            p = jnp.exp(s - mb)
            lb = jnp.sum(p, axis=-1, keepdims=True)
            ob = jnp.dot(p.astype(BF16), vc, preferred_element_type=F32)
            mb_b = jnp.broadcast_to(mb, (steps, HEAD_DIM))
            lb_b = jnp.broadcast_to(lb, (steps, HEAD_DIM))
            everything = slice(0, steps)
            if d == 1:
                for c in range(d_mid):
                    merge(pl.ds(c * L_mid + run0, run), slice(c * run, (c + 1) * run),
                          ob, mb_b, lb_b, pi == 0)
            elif d == d_mid:
                merge(pl.ds(row0, steps), everything, ob, mb_b, lb_b, pi == 0)
            else:
                start = (r % d_mid) * L_mid + r // d_mid + n * (steps * d_mid)
                merge(pl.ds(start, steps, stride=d_mid), everything, ob, mb_b, lb_b, pi == 0)
            return carry

        lax.fori_loop(0, n_blocks, body, 0, unroll=ATTN_UNROLL)

    for c in range(d_mid):
        grp = slice(c * L_mid, (c + 1) * L_mid)
        o_ref[0, pl.ds(c, L_mid, stride=d_mid), :] = acc[grp, :] / l_s[grp, :]


def _attention(qkv):
    B, S, _ = qkv.shape
    H = N_HEADS_ATTN
    blk = lambda off: pl.BlockSpec((1, S, HEAD_DIM), lambda b, h: (b, 0, off + h))
    return pl.pallas_call(
        _attn_kernel,
        out_shape=jax.ShapeDtypeStruct((B, S, D_ATTN), F32),
        grid=(B, H),
        in_specs=[blk(0), blk(H), blk(2 * H)],
        out_specs=pl.BlockSpec((1, S, HEAD_DIM), lambda b, h: (b, 0, h)),
        scratch_shapes=[pltpu.VMEM((S, HEAD_DIM), F32),
                        pltpu.VMEM((S, HEAD_DIM), F32),
                        pltpu.VMEM((S, HEAD_DIM), F32),
                        pltpu.VMEM((S, HEAD_DIM), F32),
                        pltpu.VMEM((S, HEAD_DIM), BF16),
                        pltpu.VMEM((S + ATTN_STEPS, HEAD_DIM), BF16),
                        pltpu.VMEM((S + ATTN_STEPS, HEAD_DIM), BF16),
                        pltpu.VMEM((S, HEAD_DIM), F32),
                        pltpu.VMEM((S, HEAD_DIM), F32),
                        pltpu.VMEM((S, HEAD_DIM), F32)],
        compiler_params=_params(("arbitrary", "arbitrary")),
        name="attn",
    )(qkv, qkv, qkv)


SGU_TS = 512


def _sgu_kernel(u_ref, v_ref, w_ref, b_ref, lg_ref, lb_ref, o_ref):
    C = SGU_CHUNK
    n_chunks = u_ref.shape[1] // C
    row = lax.broadcasted_iota(jnp.int32, (C, C), 0)
    col = lax.broadcasted_iota(jnp.int32, (C, C), 1)
    for h in range(N_HEADS_SGU):
        cols = slice(h * HEAD_DIM, (h + 1) * HEAD_DIM)
        v = v_ref[0, :, cols]
        mu = jnp.mean(v, axis=-1, keepdims=True)
        vc = v - mu
        var = jnp.mean(vc * vc, axis=-1, keepdims=True)
        vn = vc * lax.rsqrt(var + LN_EPS) * lg_ref[h:h + 1, :] + lb_ref[h:h + 1, :]
        vn = vn.astype(BF16)
        ws = jnp.where(row >= col, w_ref[h], 0.0).astype(BF16)
        rhs = jnp.concatenate([vn[n * C:(n + 1) * C, :] for n in range(n_chunks)], axis=1)
        mixed = jnp.dot(ws, rhs, preferred_element_type=F32)
        for n in range(n_chunks):
            gate = mixed[:, n * HEAD_DIM:(n + 1) * HEAD_DIM] + b_ref[h]
            o_ref[0, n * C:(n + 1) * C, cols] = u_ref[0, n * C:(n + 1) * C, cols] * gate


def _sgu(uv, sgu_w, sgu_b_lanes, ln_g, ln_b):
    B, S, _ = uv.shape
    ts = SGU_TS
    Hs, C = N_HEADS_SGU, SGU_CHUNK
    return pl.pallas_call(
        _sgu_kernel,
        out_shape=jax.ShapeDtypeStruct((B, S, D_SGU), F32),
        grid=(B, S // ts),
        in_specs=[pl.BlockSpec((1, ts, D_SGU), lambda b, i: (b, i, 0)),
                  pl.BlockSpec((1, ts, D_SGU), lambda b, i: (b, i, 1)),
                  pl.BlockSpec((Hs, C, C), lambda b, i: (0, 0, 0)),
                  pl.BlockSpec((Hs, C, HEAD_DIM), lambda b, i: (0, 0, 0)),
                  pl.BlockSpec((Hs, HEAD_DIM), lambda b, i: (0, 0)),
                  pl.BlockSpec((Hs, HEAD_DIM), lambda b, i: (0, 0))],
        out_specs=pl.BlockSpec((1, ts, D_SGU), lambda b, i: (b, i, 0)),
        compiler_params=_params(("arbitrary", "arbitrary")),
        name="sgu",
    )(uv, uv, sgu_w, sgu_b_lanes, ln_g, ln_b)


OUTPROJ_TM = 512


def _outproj_kernel(a_ref, s_ref, x_ref, w_ref, ga_ref, gs_ref, gt_ref, sc_ref, sh_ref, g2_ref,
                    h_ref, hn_ref, hnT_ref):
    ra = _rms(a_ref[0], RMS_EPS) * ga_ref[...]
    rs = _rms(s_ref[0], RMS_EPS) * gs_ref[...]
    mixed = jnp.concatenate([ra, rs], axis=1).astype(BF16)
    y = jnp.dot(mixed, w_ref[...], preferred_element_type=F32)
    h = x_ref[0] + gt_ref[0] * y
    h_ref[0] = h
    hn = _rms(h, RMS_EPS) * g2_ref[...]
    hn = hn * (1.0 + sc_ref[0]) + sh_ref[0]
    hn_ref[0] = hn.astype(BF16)
    hnT_ref[...] = hn.T.astype(BF16)


def _outproj(attn, sgu, x, w_out, g_attn, g_sgu, gt1, sc2, sh2, g2):
    B, S, D = x.shape
    tm = OUTPROJ_TM
    nt = S // tm
    mod = lambda: pl.BlockSpec((1, 1, D), lambda b, i: (b, 0, 0))
    return pl.pallas_call(
        _outproj_kernel,
        out_shape=(jax.ShapeDtypeStruct((B, S, D), F32),
                   jax.ShapeDtypeStruct((B, S, D), BF16),
                   jax.ShapeDtypeStruct((D, B * S), BF16)),
        grid=(B, nt),
        in_specs=[pl.BlockSpec((1, tm, D_ATTN), lambda b, i: (b, i, 0)),
                  pl.BlockSpec((1, tm, D_SGU), lambda b, i: (b, i, 0)),
                  pl.BlockSpec((1, tm, D), lambda b, i: (b, i, 0)),
                  pl.BlockSpec((D, D), lambda b, i: (0, 0)),
                  pl.BlockSpec((1, D_ATTN), lambda b, i: (0, 0)),
                  pl.BlockSpec((1, D_SGU), lambda b, i: (0, 0)),
                  mod(), mod(), mod(),
                  pl.BlockSpec((1, D), lambda b, i: (0, 0))],
        out_specs=(pl.BlockSpec((1, tm, D), lambda b, i: (b, i, 0)),
                   pl.BlockSpec((1, tm, D), lambda b, i: (b, i, 0)),
                   pl.BlockSpec((D, tm), lambda b, i: (0, b * nt + i))),
        compiler_params=_params(("arbitrary", "arbitrary")),
        name="outproj",
    )(attn, sgu, x, w_out, g_attn, g_sgu, gt1, sc2, sh2, g2)


PEERK_TM = 256
TOP_ROWS = 24
N_TOP = PEER_TOPK + 1
NO_RANK = 64.0


def _extract_top(s, count, with_rank=False):
    tops = []
    rank = jnp.full(s.shape, NO_RANK, F32) if with_rank else None
    for k in range(count):
        m = jnp.max(s, axis=0, keepdims=True)
        tops.append(m)
        hit = s == m
        if with_rank:
            rank = jnp.where(hit, float(k + 1), rank)
        s = jnp.where(hit, NEG_INF, s)
    return (tops, rank) if with_rank else tops


def _count_above(b, thr):
    rows = [b[k:k + 1] for k in range(PEER_TOPK)]
    bits = []
    span = PEER_TOPK // 2
    while span >= 1:
        level = [rows[p * 2 * span + span - 1] for p in range(2 ** len(bits))]
        for m in reversed(bits):
            level = [jnp.where(m, level[2 * i + 1], level[2 * i]) for i in range(len(level) // 2)]
        bits.append(level[0] > thr)
        span //= 2
    cnt = jnp.where(rows[PEER_TOPK - 1] > thr, 1.0, 0.0)
    for i, m in enumerate(bits):
        cnt = cnt + jnp.where(m, float(PEER_TOPK >> (i + 1)), 0.0)
    return cnt


def _peerk_kernel(hn_ref, wq_ref, sk_ref, cnt_ref, e1_ref, rk2_ref, e2_ref, q_s, s1_s, top_s):
    tm = hn_ref.shape[0]
    q_s[...] = jnp.dot(hn_ref[...], wq_ref[...], preferred_element_type=F32).astype(BF16)
    pad = jnp.full((TOP_ROWS - N_TOP, tm), NEG_INF, F32)

    def scores(hp):
        col0 = pl.multiple_of(hp * PEER_NKEYS, PEER_NKEYS)
        qh = q_s[:, pl.ds(col0, PEER_NKEYS)]
        return lax.dot_general(sk_ref[hp], qh, (((1,), (1,)), ((), ())), preferred_element_type=F32)

    def score_body(h, carry):
        s1 = scores(2 * h)
        s1_s[h] = s1
        top_s[2 * h] = jnp.concatenate(_extract_top(s1, N_TOP) + [pad], axis=0)
        s2 = scores(2 * h + 1)
        tops2, rank2 = _extract_top(s2, N_TOP, with_rank=True)
        top_s[2 * h + 1] = jnp.concatenate(tops2 + [pad], axis=0)
        rk2_ref[h] = rank2.astype(BF16)
        e2_ref[h] = jnp.exp(s2 - tops2[0]).astype(BF16)
        return carry

    lax.fori_loop(0, PEER_HEADS, score_body, 0, unroll=4)

    def head_body(h, carry):
        a = top_s[2 * h]
        b = top_s[2 * h + 1]
        groups = [a[0:1] + b]
        groups += [a[i:i + 1] + b[0:SUBLANES] for i in range(1, SUBLANES)]
        groups += [a[SUBLANES:TOP_ROWS] + b[0:1]]
        best = _extract_top(jnp.concatenate(groups, axis=0), N_TOP)
        z = jnp.zeros_like(best[0])
        for v in best[:PEER_TOPK]:
            z = z + jnp.exp(v - best[0])
        tau = 0.5 * (best[PEER_TOPK - 1] + best[PEER_TOPK])
        s1 = s1_s[h]
        thr = tau - s1
        cnt_ref[h] = _count_above(b, thr)
        e1_ref[h] = jnp.exp(s1 - a[0:1]) / z
        return carry

    lax.fori_loop(0, PEER_HEADS, head_body, 0, unroll=4)


def _peer_keys(hn2, w_q, sub_keys):
    T, D = hn2.shape
    tm = PEERK_TM
    H, K = PEER_HEADS, PEER_NKEYS
    out = jax.ShapeDtypeStruct((H, K, T), F32)
    out16 = jax.ShapeDtypeStruct((H, K, T), BF16)
    ospec = lambda: pl.BlockSpec((H, K, tm), lambda i: (0, 0, i))
    return pl.pallas_call(
        _peerk_kernel,
        out_shape=(out, out, out16, out16),
        grid=(T // tm,),
        in_specs=[pl.BlockSpec((tm, D), lambda i: (i, 0)),
                  pl.BlockSpec((D, 2 * H * K), lambda i: (0, 0)),
                  pl.BlockSpec((2 * H, K, K), lambda i: (0, 0, 0))],
        out_specs=(ospec(), ospec(), ospec(), ospec()),
        scratch_shapes=[pltpu.VMEM((tm, 2 * H * K), BF16),
                        pltpu.VMEM((H, K, tm), F32),
                        pltpu.VMEM((2 * H, TOP_ROWS, tm), F32)],
        compiler_params=_params(("arbitrary",)),
        name="peer_keys",
    )(hn2, w_q, sub_keys)


PEER_TT = 512
PEER_TE = 1024


def _peer_kernel(hT_ref, u_ref, vt_ref, cnt_ref, e1_ref, rk2_ref, e2_ref, o_ref, p0, p1, a0, a1,
                 *, chunks_per_tile):
    g = pl.program_id(0)

    @pl.when(g == 0)
    def _():
        p1[...] = jnp.zeros(p1.shape, p1.dtype)

    @pl.when(jnp.logical_or(g == 0, (g + chunks_per_tile - 1) % chunks_per_tile == 0))
    def _():
        o_ref[...] = jnp.zeros(o_ref.shape, o_ref.dtype)

    K = PEER_NKEYS
    n_c = PEER_TE // K
    d_rows = o_ref.shape[0] // n_c

    def pre_act(c, a_w):
        rows = pl.ds(pl.multiple_of(c * K, K), K)
        a_w[...] = jnp.dot(u_ref[rows, :], hT_ref[...], preferred_element_type=F32)

    def piece(c, a_r, a_w, p_w, p_r, with_next):
        zero = jnp.zeros((), BF16)
        gate = None
        for h in range(PEER_HEADS):
            cnt = cnt_ref[h, pl.ds(c, 1), :].astype(BF16)
            picked = jnp.where(rk2_ref[h] <= cnt, e2_ref[h], zero)
            term = e1_ref[h, pl.ds(c, 1), :].astype(BF16) * picked
            gate = term if gate is None else gate + term
        p_new = gate * _gelu(a_r[...]).astype(BF16)
        dr = pl.ds(pl.multiple_of(c * d_rows, d_rows), d_rows)
        o_new = o_ref[dr, :] + jnp.dot(vt_ref[dr, :], p_r[...], preferred_element_type=F32)
        if with_next:
            rows = pl.ds(pl.multiple_of((c + 1) * K, K), K)
            a_new = jnp.dot(u_ref[rows, :], hT_ref[...], preferred_element_type=F32)
        p_w[pl.ds(pl.multiple_of(c * K, K), K), :] = p_new
        o_ref[dr, :] = o_new
        if with_next:
            a_w[...] = a_new

    def step(p_w, p_r):
        pre_act(0, a0)

        def body(i, carry):
            piece(2 * i, a0, a1, p_w, p_r, True)
            piece(2 * i + 1, a1, a0, p_w, p_r, True)
            return carry

        lax.fori_loop(0, n_c // 2 - 1, body, 0)
        piece(n_c - 2, a0, a1, p_w, p_r, True)
        piece(n_c - 1, a1, a0, p_w, p_r, False)

    @pl.when(g % 2 == 0)
    def _():
        step(p0, p1)

    @pl.when(g % 2 == 1)
    def _():
        step(p1, p0)


def _peer_ffn(hn2T, u, vT, cnt, e1, rk2, e2):
    D, T = hn2T.shape
    E = u.shape[0]
    tt, te = PEER_TT, PEER_TE
    H, K = PEER_HEADS, PEER_NKEYS
    nj = E // te
    n_chunks = (T // tt) * nj
    act = lambda g: jnp.minimum(g, n_chunks - 1)
    val = lambda g: jnp.maximum(g - 1, 0)
    per_c = pl.BlockSpec((H, te // K, tt), lambda g: (0, act(g) % nj, act(g) // nj))
    per_t = pl.BlockSpec((H, K, tt), lambda g: (0, 0, act(g) // nj))
    return pl.pallas_call(
        functools.partial(_peer_kernel, chunks_per_tile=nj),
        out_shape=jax.ShapeDtypeStruct((D, T), F32),
        grid=(n_chunks + 1,),
        in_specs=[pl.BlockSpec((D, tt), lambda g: (0, act(g) // nj)),
                  pl.BlockSpec((te, D), lambda g: (act(g) % nj, 0)),
                  pl.BlockSpec((D, te), lambda g: (0, val(g) % nj)),
                  per_c, per_c, per_t, per_t],
        out_specs=pl.BlockSpec((D, tt), lambda g: (0, val(g) // nj)),
        scratch_shapes=[pltpu.VMEM((te, tt), BF16),
                        pltpu.VMEM((te, tt), BF16),
                        pltpu.VMEM((K, tt), F32),
                        pltpu.VMEM((K, tt), F32)],
        compiler_params=_params(("arbitrary",)),
        name="peer_ffn",
    )(hn2T, u, vT, cnt, e1, rk2, e2)


FINAL_TM = 512


def _final_kernel(h_ref, pT_ref, gt_ref, g_ref, o_ref):
    h = h_ref[0] + gt_ref[0] * pT_ref[...].T
    o_ref[0] = _rms(h, RMS_EPS) * g_ref[...]


def _final(h1, peerT, gt2, g_final):
    B, S, D = h1.shape
    tm = FINAL_TM
    nt = S // tm
    return pl.pallas_call(
        _final_kernel,
        out_shape=jax.ShapeDtypeStruct((B, S, D), F32),
        grid=(B, nt),
        in_specs=[pl.BlockSpec((1, tm, D), lambda b, i: (b, i, 0)),
                  pl.BlockSpec((D, tm), lambda b, i: (0, b * nt + i)),
                  pl.BlockSpec((1, 1, D), lambda b, i: (b, 0, 0)),
                  pl.BlockSpec((1, D), lambda b, i: (0, 0))],
        out_specs=pl.BlockSpec((1, tm, D), lambda b, i: (b, i, 0)),
        compiler_params=_params(("arbitrary", "arbitrary")),
        name="final",
    )(h1, peerT, gt2, g_final)


def kernel(x, c, positions, w_ada, b_ada, g_norm1, w_in, g_attn_out, g_sgu_out, sgu_w, sgu_b,
           sgu_ln_g, sgu_ln_b, w_out, g_norm2, peer_w_q, peer_sub_keys, peer_u, peer_v, g_final):
    B, S, D = x.shape
    assert w_ada.shape[0] == 1, "single-layer block"
    l = 0
    half = HEAD_DIM // 2
    inv = ROPE_THETA ** (-jnp.arange(half, dtype=F32) / half)
    inv2 = jnp.concatenate([inv, inv]).reshape(1, HEAD_DIM)
    posf = positions.astype(F32).reshape(B, S, 1)
    row = lambda g: g.reshape(1, -1)

    mod = _adaln(c, w_ada[l], b_ada[l]).reshape(B, N_MOD, 1, D)
    sh1, sc1, gt1, sh2, sc2, gt2 = [mod[:, i] for i in range(N_MOD)]

    qkv, uv = _inproj(x, posf, inv2, sc1, sh1, row(g_norm1[l]), w_in[l].astype(BF16))
    attn = _attention(qkv)
    sgu_b_lanes = jnp.broadcast_to(sgu_b[l][:, :, None], (N_HEADS_SGU, SGU_CHUNK, HEAD_DIM))
    sgu = _sgu(uv, sgu_w[l], sgu_b_lanes, sgu_ln_g[l], sgu_ln_b[l])
    h1, hn2, hn2T = _outproj(attn, sgu, x, w_out[l].astype(BF16), row(g_attn_out[l]),
                             row(g_sgu_out[l]), gt1, sc2, sh2, row(g_norm2[l]))

    sk = peer_sub_keys[l].astype(BF16).reshape(2 * PEER_HEADS, PEER_NKEYS, -1)
    cnt, e1, rk2, e2 = _peer_keys(hn2.reshape(B * S, D), peer_w_q[l].astype(BF16), sk)
    peerT = _peer_ffn(hn2T, peer_u[l].astype(BF16), peer_v[l].T.astype(BF16), cnt, e1, rk2, e2)
    return _final(h1, peerT, gt2, row(g_final))
```

```python
import functools
import math

import jax
import jax.numpy as jnp
from jax import lax
from jax.experimental import pallas as pl
from jax.experimental.pallas import tpu as pltpu

F32 = jnp.float32
BF16 = jnp.bfloat16

HEAD_DIM = 128
N_HEADS_ATTN = 12
N_HEADS_SGU = 4
D_ATTN = N_HEADS_ATTN * HEAD_DIM
D_SGU = N_HEADS_SGU * HEAD_DIM
D_QKV = 3 * D_ATTN
DILATION_PATTERNS = ((128, 1), (512, 4), (2048, 16))
ATTN_STEPS = 128
ATTN_UNROLL = 32
ROPE_THETA = 10000.0
SGU_CHUNK = 128
PEER_HEADS = 8
PEER_NKEYS = 128
PEER_TOPK = 16
N_MOD = 6
RMS_EPS = 1e-6
LN_EPS = 1e-5

LANES = 128
SUBLANES = 8
VMEM_LIMIT = 56 * 1024 * 1024
NEG_INF = float("-inf")


def _params(semantics):
    return pltpu.CompilerParams(dimension_semantics=semantics, vmem_limit_bytes=VMEM_LIMIT)


def _gelu(x):
    return 0.5 * x * (1.0 + lax.erf(x * (1.0 / math.sqrt(2.0))))


def _gelu_doubled(x):
    return x * (1.0 + lax.erf(x * (1.0 / math.sqrt(2.0))))


def _rms(x, eps):
    return x * lax.rsqrt(jnp.mean(x * x, axis=-1, keepdims=True) + eps)


def _adaln_kernel(c_ref, w_ref, b_ref, o_ref):
    c = c_ref[...]
    cond = c * jax.nn.sigmoid(c)
    o_ref[...] = jnp.dot(cond.astype(BF16), w_ref[...].astype(BF16),
                         preferred_element_type=F32) + b_ref[...]


def _adaln(c, w_ada, b_ada):
    B, D = c.shape
    N = w_ada.shape[1]
    tn = 1024
    rows = SUBLANES
    c_pad = jnp.pad(c, ((0, rows - B), (0, 0)))
    out = pl.pallas_call(
        _adaln_kernel,
        out_shape=jax.ShapeDtypeStruct((rows, N), F32),
        grid=(N // tn,),
        in_specs=[pl.BlockSpec((rows, D), lambda j: (0, 0)),
                  pl.BlockSpec((D, tn), lambda j: (0, j)),
                  pl.BlockSpec((1, tn), lambda j: (0, j))],
        out_specs=pl.BlockSpec((rows, tn), lambda j: (0, j)),
        compiler_params=_params(("arbitrary",)),
        name="adaln",
    )(c_pad, w_ada, b_ada.reshape(1, N))
    return out[:B]


INPROJ_TM = 1024
INPROJ_TN = 512
_QK_TILES = 2 * D_ATTN // INPROJ_TN
_Q_TILES = D_ATTN // INPROJ_TN
_QKV_TILES = D_QKV // INPROJ_TN


def _inproj_kernel(x_ref, pos_ref, inv_ref, sc_ref, sh_ref, g_ref, w_ref,
                   qkv_ref, uv_ref, hn_ref, cos_ref, sin_ref):
    j = pl.program_id(2)

    @pl.when(j == 0)
    def _():
        x = x_ref[0]
        hn = _rms(x, RMS_EPS) * g_ref[...]
        hn = hn * (1.0 + sc_ref[0]) + sh_ref[0]
        hn_ref[...] = hn.astype(BF16)
        ang = pos_ref[0] * inv_ref[...]
        lane = lax.broadcasted_iota(jnp.int32, ang.shape, 1)
        cos_ref[...] = jnp.cos(ang)
        sin_ref[...] = jnp.where(lane < HEAD_DIM // 2, -1.0, 1.0) * jnp.sin(ang)

    def project():
        return jnp.dot(hn_ref[...], w_ref[...], preferred_element_type=F32)

    @pl.when(j < _QK_TILES)
    def _():
        scale = jnp.where(j < _Q_TILES, HEAD_DIM ** -0.5, 1.0).astype(F32)
        cos = cos_ref[...] * scale
        sin = sin_ref[...] * scale
        acc = project()
        for h in range(INPROJ_TN // HEAD_DIM):
            cols = slice(h * HEAD_DIM, (h + 1) * HEAD_DIM)
            t = acc[:, cols]
            r = t * cos + pltpu.roll(t, HEAD_DIM // 2, 1) * sin
            qkv_ref[0, :, cols] = r.astype(BF16)

    @pl.when(jnp.logical_and(j >= _QK_TILES, j < _QKV_TILES))
    def _():
        qkv_ref[0] = project().astype(BF16)

    @pl.when(j >= _QKV_TILES)
    def _():
        uv_ref[0] = _gelu(project())


def _inproj(x, posf, inv2, sc1, sh1, g1, w_in):
    B, S, D = x.shape
    N = w_in.shape[1]
    tm, tn = INPROJ_TM, INPROJ_TN
    last_qkv = _QKV_TILES - 1
    return pl.pallas_call(
        _inproj_kernel,
        out_shape=(jax.ShapeDtypeStruct((B, S, D_QKV), BF16),
                   jax.ShapeDtypeStruct((B, S, 2 * D_SGU), F32)),
        grid=(B, S // tm, N // tn),
        in_specs=[pl.BlockSpec((1, tm, D), lambda b, i, j: (b, i, 0)),
                  pl.BlockSpec((1, tm, 1), lambda b, i, j: (b, i, 0)),
                  pl.BlockSpec((1, LANES), lambda b, i, j: (0, 0)),
                  pl.BlockSpec((1, 1, D), lambda b, i, j: (b, 0, 0)),
                  pl.BlockSpec((1, 1, D), lambda b, i, j: (b, 0, 0)),
                  pl.BlockSpec((1, D), lambda b, i, j: (0, 0)),
                  pl.BlockSpec((D, tn), lambda b, i, j: (0, j))],
        out_specs=(pl.BlockSpec((1, tm, tn), lambda b, i, j: (b, i, jnp.minimum(j, last_qkv))),
                   pl.BlockSpec((1, tm, tn), lambda b, i, j: (b, i, jnp.maximum(j - _QKV_TILES, 0)))),
        scratch_shapes=[pltpu.VMEM((tm, D), BF16),
                        pltpu.VMEM((tm, LANES), F32),
                        pltpu.VMEM((tm, LANES), F32)],
        compiler_params=_params(("arbitrary", "arbitrary", "arbitrary")),
        name="inproj",
    )(x, posf, inv2, sc1, sh1, g1, w_in)


def _attn_kernel(q_ref, k_ref, v_ref, o_ref, nat, q4, k4, v4, qd, kd, vd, acc, m_s, l_s):
    S = q_ref.shape[1]
    steps = ATTN_STEPS
    n_blocks = S // steps
    (_, d_min), (_, d_mid), (_, d_max) = DILATION_PATTERNS
    assert d_min == 1 and d_max == d_mid * d_mid
    L_mid = S // d_mid
    for src_ref, grouped in ((q_ref, q4), (k_ref, k4), (v_ref, v4)):
        nat[...] = src_ref[0].astype(F32)
        for r in range(d_mid):
            grouped[r * L_mid:(r + 1) * L_mid, :] = nat[pl.ds(r, L_mid, stride=d_mid), :]
    kd[0:steps, :] = jnp.zeros((steps, HEAD_DIM), BF16)
    vd[0:steps, :] = jnp.zeros((steps, HEAD_DIM), BF16)

    row = lax.broadcasted_iota(jnp.int32, (steps, 2 * steps), 0)
    km = lax.broadcasted_iota(jnp.int32, (steps, 2 * steps), 1)
    band = jnp.logical_and(km >= row, km <= row + steps)
    run = steps // d_mid
    perm = (row % run) * d_mid + row // run
    band_grouped = jnp.logical_and(km >= perm, km <= perm + steps)

    def merge(rows, sl, ob, mb_b, lb_b, init):
        if init:
            acc[rows, :] = ob[sl]
            m_s[rows, :] = mb_b[sl]
            l_s[rows, :] = lb_b[sl]
        else:
            m_old = m_s[rows, :]
            m_new = jnp.maximum(m_old, mb_b[sl])
            a_old = jnp.exp(m_old - m_new)
            a_blk = jnp.exp(mb_b[sl] - m_new)
            acc[rows, :] = acc[rows, :] * a_old + ob[sl] * a_blk
            l_s[rows, :] = l_s[rows, :] * a_old + lb_b[sl] * a_blk
            m_s[rows, :] = m_new

    for pi, (window, d) in enumerate(reversed(DILATION_PATTERNS)):
        assert window // d == steps
        L = S // d
        nb = L // steps
        if d == 1:
            kd[steps:, :] = k_ref[0]
            vd[steps:, :] = v_ref[0]
        elif d == d_mid:
            for src, dst, off in ((q4, qd, 0), (k4, kd, steps), (v4, vd, steps)):
                dst[off:off + S, :] = src[...].astype(BF16)
        else:
            for r in range(d):
                start = (r % d_mid) * L_mid + r // d_mid
                for src, dst, off in ((q4, qd, 0), (k4, kd, steps), (v4, vd, steps)):
                    dst[off + r * L:off + (r + 1) * L, :] = src[pl.ds(start, L, stride=d_mid), :].astype(BF16)

        def body(g, carry, pi=pi, d=d, nb=nb):
            n = g % nb
            r = g // nb
            row0 = pl.multiple_of(g * steps, steps)
            if d == 1:
                run0 = pl.multiple_of(g * run, run)
                qb = jnp.concatenate([q4[pl.ds(c * L_mid + run0, run), :] for c in range(d_mid)],
                                     axis=0).astype(BF16)
            else:
                qb = qd[pl.ds(row0, steps), :]
            kc = kd[pl.ds(row0, 2 * steps), :]
            vc = vd[pl.ds(row0, 2 * steps), :]
            s = lax.dot_general(qb, kc, (((1,), (1,)), ((), ())), preferred_element_type=F32)
            first_key = jnp.where(n > 0, 0, steps)
            in_band = band_grouped if d == 1 else band
            s = jnp.where(jnp.logical_and(in_band, km >= first_key), s, NEG_INF)
            mb = jnp.max(s, axis=-1, keepdims=True)
            p = jnp.exp(s - mb)
            lb = jnp.sum(p, axis=-1, keepdims=True)
            ob = jnp.dot(p.astype(BF16), vc, preferred_element_type=F32)
            mb_b = jnp.broadcast_to(mb, (steps, HEAD_DIM))
            lb_b = jnp.broadcast_to(lb, (steps, HEAD_DIM))
            everything = slice(0, steps)
            if d == 1:
                for c in range(d_mid):
                    merge(pl.ds(c * L_mid + run0, run), slice(c * run, (c + 1) * run),
                          ob, mb_b, lb_b, pi == 0)
            elif d == d_mid:
                merge(pl.ds(row0, steps), everything, ob, mb_b, lb_b, pi == 0)
            else:
                start = (r % d_mid) * L_mid + r // d_mid + n * (steps * d_mid)
                merge(pl.ds(start, steps, stride=d_mid), everything, ob, mb_b, lb_b, pi == 0)
            return carry

        lax.fori_loop(0, n_blocks, body, 0, unroll=ATTN_UNROLL)

    for c in range(d_mid):
        grp = slice(c * L_mid, (c + 1) * L_mid)
        o_ref[0, pl.ds(c, L_mid, stride=d_mid), :] = acc[grp, :] / l_s[grp, :]


def _attention(qkv):
    B, S, _ = qkv.shape
    H = N_HEADS_ATTN
    blk = lambda off: pl.BlockSpec((1, S, HEAD_DIM), lambda b, h: (b, 0, off + h))
    return pl.pallas_call(
        _attn_kernel,
        out_shape=jax.ShapeDtypeStruct((B, S, D_ATTN), F32),
        grid=(B, H),
        in_specs=[blk(0), blk(H), blk(2 * H)],
        out_specs=pl.BlockSpec((1, S, HEAD_DIM), lambda b, h: (b, 0, h)),
        scratch_shapes=[pltpu.VMEM((S, HEAD_DIM), F32),
                        pltpu.VMEM((S, HEAD_DIM), F32),
                        pltpu.VMEM((S, HEAD_DIM), F32),
                        pltpu.VMEM((S, HEAD_DIM), F32),
                        pltpu.VMEM((S, HEAD_DIM), BF16),
                        pltpu.VMEM((S + ATTN_STEPS, HEAD_DIM), BF16),
                        pltpu.VMEM((S + ATTN_STEPS, HEAD_DIM), BF16),
                        pltpu.VMEM((S, HEAD_DIM), F32),
                        pltpu.VMEM((S, HEAD_DIM), F32),
                        pltpu.VMEM((S, HEAD_DIM), F32)],
        compiler_params=_params(("arbitrary", "arbitrary")),
        name="attn",
    )(qkv, qkv, qkv)


SGU_TS = 512


def _sgu_kernel(u_ref, v_ref, w_ref, b_ref, lg_ref, lb_ref, o_ref):
    C = SGU_CHUNK
    n_chunks = u_ref.shape[1] // C
    row = lax.broadcasted_iota(jnp.int32, (C, C), 0)
    col = lax.broadcasted_iota(jnp.int32, (C, C), 1)
    for h in range(N_HEADS_SGU):
        cols = slice(h * HEAD_DIM, (h + 1) * HEAD_DIM)
        v = v_ref[0, :, cols]
        mu = jnp.mean(v, axis=-1, keepdims=True)
        vc = v - mu
        var = jnp.mean(vc * vc, axis=-1, keepdims=True)
        vn = vc * lax.rsqrt(var + LN_EPS) * lg_ref[h:h + 1, :] + lb_ref[h:h + 1, :]
        vn = vn.astype(BF16)
        ws = jnp.where(row >= col, w_ref[h], 0.0).astype(BF16)
        rhs = jnp.concatenate([vn[n * C:(n + 1) * C, :] for n in range(n_chunks)], axis=1)
        mixed = jnp.dot(ws, rhs, preferred_element_type=F32)
        for n in range(n_chunks):
            gate = mixed[:, n * HEAD_DIM:(n + 1) * HEAD_DIM] + b_ref[h]
            o_ref[0, n * C:(n + 1) * C, cols] = u_ref[0, n * C:(n + 1) * C, cols] * gate


def _sgu(uv, sgu_w, sgu_b_lanes, ln_g, ln_b):
    B, S, _ = uv.shape
    ts = SGU_TS
    Hs, C = N_HEADS_SGU, SGU_CHUNK
    return pl.pallas_call(
        _sgu_kernel,
        out_shape=jax.ShapeDtypeStruct((B, S, D_SGU), F32),
        grid=(B, S // ts),
        in_specs=[pl.BlockSpec((1, ts, D_SGU), lambda b, i: (b, i, 0)),
                  pl.BlockSpec((1, ts, D_SGU), lambda b, i: (b, i, 1)),
                  pl.BlockSpec((Hs, C, C), lambda b, i: (0, 0, 0)),
                  pl.BlockSpec((Hs, C, HEAD_DIM), lambda b, i: (0, 0, 0)),
                  pl.BlockSpec((Hs, HEAD_DIM), lambda b, i: (0, 0)),
                  pl.BlockSpec((Hs, HEAD_DIM), lambda b, i: (0, 0))],
        out_specs=pl.BlockSpec((1, ts, D_SGU), lambda b, i: (b, i, 0)),
        compiler_params=_params(("arbitrary", "arbitrary")),
        name="sgu",
    )(uv, uv, sgu_w, sgu_b_lanes, ln_g, ln_b)


OUTPROJ_TM = 512


def _outproj_kernel(a_ref, s_ref, x_ref, w_ref, ga_ref, gs_ref, gt_ref, sc_ref, sh_ref, g2_ref,
                    h_ref, hn_ref, hnT_ref):
    ra = _rms(a_ref[0], RMS_EPS) * ga_ref[...]
    rs = _rms(s_ref[0], RMS_EPS) * gs_ref[...]
    mixed = jnp.concatenate([ra, rs], axis=1).astype(BF16)
    y = jnp.dot(mixed, w_ref[...], preferred_element_type=F32)
    h = x_ref[0] + gt_ref[0] * y
    h_ref[0] = h
    hn = _rms(h, RMS_EPS) * g2_ref[...]
    hn = hn * (1.0 + sc_ref[0]) + sh_ref[0]
    hn_ref[0] = hn.astype(BF16)
    hnT_ref[...] = hn.T.astype(BF16)


def _outproj(attn, sgu, x, w_out, g_attn, g_sgu, gt1, sc2, sh2, g2):
    B, S, D = x.shape
    tm = OUTPROJ_TM
    nt = S // tm
    mod = lambda: pl.BlockSpec((1, 1, D), lambda b, i: (b, 0, 0))
    return pl.pallas_call(
        _outproj_kernel,
        out_shape=(jax.ShapeDtypeStruct((B, S, D), F32),
                   jax.ShapeDtypeStruct((B, S, D), BF16),
                   jax.ShapeDtypeStruct((D, B * S), BF16)),
        grid=(B, nt),
        in_specs=[pl.BlockSpec((1, tm, D_ATTN), lambda b, i: (b, i, 0)),
                  pl.BlockSpec((1, tm, D_SGU), lambda b, i: (b, i, 0)),
                  pl.BlockSpec((1, tm, D), lambda b, i: (b, i, 0)),
                  pl.BlockSpec((D, D), lambda b, i: (0, 0)),
                  pl.BlockSpec((1, D_ATTN), lambda b, i: (0, 0)),
                  pl.BlockSpec((1, D_SGU), lambda b, i: (0, 0)),
                  mod(), mod(), mod(),
                  pl.BlockSpec((1, D), lambda b, i: (0, 0))],
        out_specs=(pl.BlockSpec((1, tm, D), lambda b, i: (b, i, 0)),
                   pl.BlockSpec((1, tm, D), lambda b, i: (b, i, 0)),
                   pl.BlockSpec((D, tm), lambda b, i: (0, b * nt + i))),
        compiler_params=_params(("arbitrary", "arbitrary")),
        name="outproj",
    )(attn, sgu, x, w_out, g_attn, g_sgu, gt1, sc2, sh2, g2)


PEERK_TM = 256
TOP_ROWS = 24
N_TOP = PEER_TOPK + 1
NO_RANK = 64.0


def _extract_top(s, count, with_rank=False):
    tops = []
    rank = jnp.full(s.shape, NO_RANK, F32) if with_rank else None
    for k in range(count):
        m = jnp.max(s, axis=0, keepdims=True)
        tops.append(m)
        hit = s == m
        if with_rank:
            rank = jnp.where(hit, float(k + 1), rank)
        s = jnp.where(hit, NEG_INF, s)
    return (tops, rank) if with_rank else tops


def _count_above(b, thr):
    rows = [b[k:k + 1] for k in range(PEER_TOPK)]
    bits = []
    span = PEER_TOPK // 2
    while span >= 1:
        level = [rows[p * 2 * span + span - 1] for p in range(2 ** len(bits))]
        for m in reversed(bits):
            level = [jnp.where(m, level[2 * i + 1], level[2 * i]) for i in range(len(level) // 2)]
        bits.append(level[0] > thr)
        span //= 2
    cnt = jnp.where(rows[PEER_TOPK - 1] > thr, 1.0, 0.0)
    for i, m in enumerate(bits):
        cnt = cnt + jnp.where(m, float(PEER_TOPK >> (i + 1)), 0.0)
    return cnt


def _peerk_kernel(hn_ref, wq_ref, sk_ref, cnt_ref, e1_ref, rk2_ref, e2_ref, q_s, s1_s, top_s):
    tm = hn_ref.shape[0]
    q_s[...] = jnp.dot(hn_ref[...], wq_ref[...], preferred_element_type=F32).astype(BF16)
    pad = jnp.full((TOP_ROWS - N_TOP, tm), NEG_INF, F32)

    def scores(hp):
        col0 = pl.multiple_of(hp * PEER_NKEYS, PEER_NKEYS)
        qh = q_s[:, pl.ds(col0, PEER_NKEYS)]
        return lax.dot_general(sk_ref[hp], qh, (((1,), (1,)), ((), ())), preferred_element_type=F32)

    def score_body(h, carry):
        s1 = scores(2 * h)
        s1_s[h] = s1
        top_s[2 * h] = jnp.concatenate(_extract_top(s1, N_TOP) + [pad], axis=0)
        s2 = scores(2 * h + 1)
        tops2, rank2 = _extract_top(s2, N_TOP, with_rank=True)
        top_s[2 * h + 1] = jnp.concatenate(tops2 + [pad], axis=0)
        rk2_ref[h] = rank2.astype(BF16)
        e2_ref[h] = jnp.exp(s2 - tops2[0]).astype(BF16)
        return carry

    lax.fori_loop(0, PEER_HEADS, score_body, 0, unroll=4)

    def head_body(h, carry):
        a = top_s[2 * h]
        b = top_s[2 * h + 1]
        groups = [a[0:1] + b]
        groups += [a[i:i + 1] + b[0:SUBLANES] for i in range(1, SUBLANES)]
        groups += [a[SUBLANES:TOP_ROWS] + b[0:1]]
        best = _extract_top(jnp.concatenate(groups, axis=0), N_TOP)
        z = jnp.zeros_like(best[0])
        for v in best[:PEER_TOPK]:
            z = z + jnp.exp(v - best[0])
        tau = 0.5 * (best[PEER_TOPK - 1] + best[PEER_TOPK])
        s1 = s1_s[h]
        thr = tau - s1
        cnt_ref[h] = _count_above(b, thr)
        e1_ref[h] = jnp.exp(s1 - a[0:1]) * (0.5 / z)
        return carry

    lax.fori_loop(0, PEER_HEADS, head_body, 0, unroll=4)


def _peer_keys(hn2, w_q, sub_keys):
    T, D = hn2.shape
    tm = PEERK_TM
    H, K = PEER_HEADS, PEER_NKEYS
    out = jax.ShapeDtypeStruct((H, K, T), F32)
    out16 = jax.ShapeDtypeStruct((H, K, T), BF16)
    ospec = lambda: pl.BlockSpec((H, K, tm), lambda i: (0, 0, i))
    return pl.pallas_call(
        _peerk_kernel,
        out_shape=(out, out, out16, out16),
        grid=(T // tm,),
        in_specs=[pl.BlockSpec((tm, D), lambda i: (i, 0)),
                  pl.BlockSpec((D, 2 * H * K), lambda i: (0, 0)),
                  pl.BlockSpec((2 * H, K, K), lambda i: (0, 0, 0))],
        out_specs=(ospec(), ospec(), ospec(), ospec()),
        scratch_shapes=[pltpu.VMEM((tm, 2 * H * K), BF16),
                        pltpu.VMEM((H, K, tm), F32),
                        pltpu.VMEM((2 * H, TOP_ROWS, tm), F32)],
        compiler_params=_params(("arbitrary",)),
        name="peer_keys",
    )(hn2, w_q, sub_keys)


PEER_TT = 512
PEER_TE = 1024


def _peer_kernel(hT_ref, u_ref, vt_ref, cnt_ref, e1_ref, rk2_ref, e2_ref, h1_ref, gt_ref, gf_ref,
                 out_ref, p0, p1, a0, a1, o_s, *, chunks_per_tile):
    g = pl.program_id(0)

    @pl.when(g == 0)
    def _():
        p1[...] = jnp.zeros(p1.shape, p1.dtype)

    @pl.when(jnp.logical_or(g == 0, (g + chunks_per_tile - 1) % chunks_per_tile == 0))
    def _():
        o_s[...] = jnp.zeros(o_s.shape, o_s.dtype)

    K = PEER_NKEYS
    n_c = PEER_TE // K
    d_rows = o_s.shape[0] // n_c

    def pre_act(c, a_w):
        rows = pl.ds(pl.multiple_of(c * K, K), K)
        a_w[...] = jnp.dot(u_ref[rows, :], hT_ref[...], preferred_element_type=F32)

    def piece(c, a_r, a_w, p_w, p_r, with_next):
        zero = jnp.zeros((), BF16)
        gate = None
        for h in range(PEER_HEADS):
            cnt = cnt_ref[h, pl.ds(c, 1), :].astype(BF16)
            picked = jnp.where(rk2_ref[h] <= cnt, e2_ref[h], zero)
            term = e1_ref[h, pl.ds(c, 1), :].astype(BF16) * picked
            gate = term if gate is None else gate + term
        p_new = gate * _gelu_doubled(a_r[...]).astype(BF16)
        dr = pl.ds(pl.multiple_of(c * d_rows, d_rows), d_rows)
        o_new = o_s[dr, :] + jnp.dot(vt_ref[dr, :], p_r[...], preferred_element_type=F32)
        if with_next:
            rows = pl.ds(pl.multiple_of((c + 1) * K, K), K)
            a_new = jnp.dot(u_ref[rows, :], hT_ref[...], preferred_element_type=F32)
        p_w[pl.ds(pl.multiple_of(c * K, K), K), :] = p_new
        o_s[dr, :] = o_new
        if with_next:
            a_w[...] = a_new

    def step(p_w, p_r):
        pre_act(0, a0)

        def body(i, carry):
            piece(2 * i, a0, a1, p_w, p_r, True)
            piece(2 * i + 1, a1, a0, p_w, p_r, True)
            return carry

        lax.fori_loop(0, n_c // 2 - 1, body, 0)
        piece(n_c - 2, a0, a1, p_w, p_r, True)
        piece(n_c - 1, a1, a0, p_w, p_r, False)

    @pl.when(g % 2 == 0)
    def _():
        step(p0, p1)

    @pl.when(g % 2 == 1)
    def _():
        step(p1, p0)

    @pl.when(jnp.logical_and(g > 0, g % chunks_per_tile == 0))
    def _():
        h = h1_ref[...] + gt_ref[0] * o_s[...].T
        out_ref[...] = _rms(h, RMS_EPS) * gf_ref[...]


def _peer_ffn(hn2T, u, vT, cnt, e1, rk2, e2, h1, gt2, g_final):
    D, T = hn2T.shape
    E = u.shape[0]
    tt, te = PEER_TT, PEER_TE
    H, K = PEER_HEADS, PEER_NKEYS
    nj = E // te
    n_chunks = (T // tt) * nj
    tiles_per_batch = T // gt2.shape[0] // tt
    act = lambda g: jnp.minimum(g, n_chunks - 1)
    val_tile = lambda g: jnp.maximum(g - 1, 0) // nj
    per_c = pl.BlockSpec((H, te // K, tt), lambda g: (0, act(g) % nj, act(g) // nj))
    per_t = pl.BlockSpec((H, K, tt), lambda g: (0, 0, act(g) // nj))
    return pl.pallas_call(
        functools.partial(_peer_kernel, chunks_per_tile=nj),
        out_shape=jax.ShapeDtypeStruct((T, D), F32),
        grid=(n_chunks + 1,),
        in_specs=[pl.BlockSpec((D, tt), lambda g: (0, act(g) // nj)),
                  pl.BlockSpec((te, D), lambda g: (act(g) % nj, 0)),
                  pl.BlockSpec((D, te), lambda g: (0, jnp.maximum(g - 1, 0) % nj)),
                  per_c, per_c, per_t, per_t,
                  pl.BlockSpec((tt, D), lambda g: (val_tile(g), 0)),
                  pl.BlockSpec((1, 1, D), lambda g: (val_tile(g) // tiles_per_batch, 0, 0)),
                  pl.BlockSpec((1, D), lambda g: (0, 0))],
        out_specs=pl.BlockSpec((tt, D), lambda g: (val_tile(g), 0)),
        scratch_shapes=[pltpu.VMEM((te, tt), BF16),
                        pltpu.VMEM((te, tt), BF16),
                        pltpu.VMEM((K, tt), F32),
                        pltpu.VMEM((K, tt), F32),
                        pltpu.VMEM((D, tt), F32)],
        compiler_params=_params(("arbitrary",)),
        name="peer_ffn",
    )(hn2T, u, vT, cnt, e1, rk2, e2, h1, gt2, g_final)


def kernel(x, c, positions, w_ada, b_ada, g_norm1, w_in, g_attn_out, g_sgu_out, sgu_w, sgu_b,
           sgu_ln_g, sgu_ln_b, w_out, g_norm2, peer_w_q, peer_sub_keys, peer_u, peer_v, g_final):
    B, S, D = x.shape
    assert w_ada.shape[0] == 1, "single-layer block"
    l = 0
    half = HEAD_DIM // 2
    inv = ROPE_THETA ** (-jnp.arange(half, dtype=F32) / half)
    inv2 = jnp.concatenate([inv, inv]).reshape(1, HEAD_DIM)
    posf = positions.astype(F32).reshape(B, S, 1)
    row = lambda g: g.reshape(1, -1)

    mod = _adaln(c, w_ada[l], b_ada[l]).reshape(B, N_MOD, 1, D)
    sh1, sc1, gt1, sh2, sc2, gt2 = [mod[:, i] for i in range(N_MOD)]

    qkv, uv = _inproj(x, posf, inv2, sc1, sh1, row(g_norm1[l]), w_in[l].astype(BF16))
    attn = _attention(qkv)
    sgu_b_lanes = jnp.broadcast_to(sgu_b[l][:, :, None], (N_HEADS_SGU, SGU_CHUNK, HEAD_DIM))
    sgu = _sgu(uv, sgu_w[l], sgu_b_lanes, sgu_ln_g[l], sgu_ln_b[l])
    h1, hn2, hn2T = _outproj(attn, sgu, x, w_out[l].astype(BF16), row(g_attn_out[l]),
                             row(g_sgu_out[l]), gt1, sc2, sh2, row(g_norm2[l]))

    sk = peer_sub_keys[l].astype(BF16).reshape(2 * PEER_HEADS, PEER_NKEYS, -1)
    cnt, e1, rk2, e2 = _peer_keys(hn2.reshape(B * S, D), peer_w_q[l].astype(BF16), sk)
    out = _peer_ffn(hn2T, peer_u[l].astype(BF16), peer_v[l].T.astype(BF16), cnt, e1, rk2, e2,
                    h1.reshape(B * S, D), gt2, row(g_final))
    return out.reshape(B, S, D)
```

```python
import functools
import math

import jax
import jax.numpy as jnp
from jax import lax
from jax.experimental import pallas as pl
from jax.experimental.pallas import tpu as pltpu

F32 = jnp.float32
BF16 = jnp.bfloat16

HEAD_DIM = 128
N_HEADS_ATTN = 12
N_HEADS_SGU = 4
D_ATTN = N_HEADS_ATTN * HEAD_DIM
D_SGU = N_HEADS_SGU * HEAD_DIM
D_QKV = 3 * D_ATTN
DILATION_PATTERNS = ((128, 1), (512, 4), (2048, 16))
ATTN_STEPS = 128
ATTN_UNROLL = 32
ROPE_THETA = 10000.0
SGU_CHUNK = 128
PEER_HEADS = 8
PEER_NKEYS = 128
PEER_TOPK = 16
N_MOD = 6
RMS_EPS = 1e-6
LN_EPS = 1e-5

LANES = 128
SUBLANES = 8
VMEM_LIMIT = 56 * 1024 * 1024
NEG_INF = float("-inf")


def _params(semantics):
    return pltpu.CompilerParams(dimension_semantics=semantics, vmem_limit_bytes=VMEM_LIMIT)


def _gelu(x):
    return 0.5 * x * (1.0 + lax.erf(x * (1.0 / math.sqrt(2.0))))


def _gelu_doubled(x):
    return x * (1.0 + lax.erf(x * (1.0 / math.sqrt(2.0))))


def _rms(x, eps):
    return x * lax.rsqrt(jnp.mean(x * x, axis=-1, keepdims=True) + eps)


def _adaln_kernel(c_ref, w_ref, b_ref, o_ref):
    c = c_ref[...]
    cond = c * jax.nn.sigmoid(c)
    o_ref[...] = jnp.dot(cond.astype(BF16), w_ref[...].astype(BF16),
                         preferred_element_type=F32) + b_ref[...]


def _adaln(c, w_ada, b_ada):
    B, D = c.shape
    N = w_ada.shape[1]
    tn = 1024
    rows = SUBLANES
    c_pad = jnp.pad(c, ((0, rows - B), (0, 0)))
    out = pl.pallas_call(
        _adaln_kernel,
        out_shape=jax.ShapeDtypeStruct((rows, N), F32),
        grid=(N // tn,),
        in_specs=[pl.BlockSpec((rows, D), lambda j: (0, 0)),
                  pl.BlockSpec((D, tn), lambda j: (0, j)),
                  pl.BlockSpec((1, tn), lambda j: (0, j))],
        out_specs=pl.BlockSpec((rows, tn), lambda j: (0, j)),
        compiler_params=_params(("arbitrary",)),
        name="adaln",
    )(c_pad, w_ada, b_ada.reshape(1, N))
    return out[:B]


INPROJ_TM = 1024
INPROJ_TN = 512
_QK_TILES = 2 * D_ATTN // INPROJ_TN
_Q_TILES = D_ATTN // INPROJ_TN
_QKV_TILES = D_QKV // INPROJ_TN


def _inproj_kernel(x_ref, pos_ref, inv_ref, sc_ref, sh_ref, g_ref, w_ref,
                   qkv_ref, uv_ref, hn_ref, cos_ref, sin_ref):
    j = pl.program_id(2)

    @pl.when(j == 0)
    def _():
        x = x_ref[0]
        hn = _rms(x, RMS_EPS) * g_ref[...]
        hn = hn * (1.0 + sc_ref[0]) + sh_ref[0]
        hn_ref[...] = hn.astype(BF16)
        ang = pos_ref[0] * inv_ref[...]
        lane = lax.broadcasted_iota(jnp.int32, ang.shape, 1)
        cos_ref[...] = jnp.cos(ang)
        sin_ref[...] = jnp.where(lane < HEAD_DIM // 2, -1.0, 1.0) * jnp.sin(ang)

    def project():
        return jnp.dot(hn_ref[...], w_ref[...], preferred_element_type=F32)

    @pl.when(j < _QK_TILES)
    def _():
        scale = jnp.where(j < _Q_TILES, HEAD_DIM ** -0.5, 1.0).astype(F32)
        cos = cos_ref[...] * scale
        sin = sin_ref[...] * scale
        acc = project()
        for h in range(INPROJ_TN // HEAD_DIM):
            cols = slice(h * HEAD_DIM, (h + 1) * HEAD_DIM)
            t = acc[:, cols]
            r = t * cos + pltpu.roll(t, HEAD_DIM // 2, 1) * sin
            qkv_ref[0, :, cols] = r.astype(BF16)

    @pl.when(jnp.logical_and(j >= _QK_TILES, j < _QKV_TILES))
    def _():
        qkv_ref[0] = project().astype(BF16)

    @pl.when(j >= _QKV_TILES)
    def _():
        uv_ref[0] = _gelu(project())


def _inproj(x, posf, inv2, sc1, sh1, g1, w_in):
    B, S, D = x.shape
    N = w_in.shape[1]
    tm, tn = INPROJ_TM, INPROJ_TN
    last_qkv = _QKV_TILES - 1
    return pl.pallas_call(
        _inproj_kernel,
        out_shape=(jax.ShapeDtypeStruct((B, S, D_QKV), BF16),
                   jax.ShapeDtypeStruct((B, S, 2 * D_SGU), F32)),
        grid=(B, S // tm, N // tn),
        in_specs=[pl.BlockSpec((1, tm, D), lambda b, i, j: (b, i, 0)),
                  pl.BlockSpec((1, tm, 1), lambda b, i, j: (b, i, 0)),
                  pl.BlockSpec((1, LANES), lambda b, i, j: (0, 0)),
                  pl.BlockSpec((1, 1, D), lambda b, i, j: (b, 0, 0)),
                  pl.BlockSpec((1, 1, D), lambda b, i, j: (b, 0, 0)),
                  pl.BlockSpec((1, D), lambda b, i, j: (0, 0)),
                  pl.BlockSpec((D, tn), lambda b, i, j: (0, j))],
        out_specs=(pl.BlockSpec((1, tm, tn), lambda b, i, j: (b, i, jnp.minimum(j, last_qkv))),
                   pl.BlockSpec((1, tm, tn), lambda b, i, j: (b, i, jnp.maximum(j - _QKV_TILES, 0)))),
        scratch_shapes=[pltpu.VMEM((tm, D), BF16),
                        pltpu.VMEM((tm, LANES), F32),
                        pltpu.VMEM((tm, LANES), F32)],
        compiler_params=_params(("arbitrary", "arbitrary", "arbitrary")),
        name="inproj",
    )(x, posf, inv2, sc1, sh1, g1, w_in)


def _attn_kernel(q_ref, k_ref, v_ref, o_ref, nat, q4, k4, v4, qd, kd, vd, acc, m_s, l_s):
    S = q_ref.shape[1]
    steps = ATTN_STEPS
    n_blocks = S // steps
    (_, d_min), (_, d_mid), (_, d_max) = DILATION_PATTERNS
    assert d_min == 1 and d_max == d_mid * d_mid
    L_mid = S // d_mid
    for src_ref, grouped in ((q_ref, q4), (k_ref, k4), (v_ref, v4)):
        nat[...] = src_ref[0].astype(F32)
        for r in range(d_mid):
            grouped[r * L_mid:(r + 1) * L_mid, :] = nat[pl.ds(r, L_mid, stride=d_mid), :]
    kd[0:steps, :] = jnp.zeros((steps, HEAD_DIM), BF16)
    vd[0:steps, :] = jnp.zeros((steps, HEAD_DIM), BF16)

    row = lax.broadcasted_iota(jnp.int32, (steps, 2 * steps), 0)
    km = lax.broadcasted_iota(jnp.int32, (steps, 2 * steps), 1)
    band = jnp.logical_and(km >= row, km <= row + steps)
    run = steps // d_mid
    perm = (row % run) * d_mid + row // run
    band_grouped = jnp.logical_and(km >= perm, km <= perm + steps)

    def merge(rows, sl, ob, mb_b, lb_b, init):
        if init:
            acc[rows, :] = ob[sl]
            m_s[rows, :] = mb_b[sl]
            l_s[rows, :] = lb_b[sl]
        else:
            m_old = m_s[rows, :]
            m_new = jnp.maximum(m_old, mb_b[sl])
            a_old = jnp.exp(m_old - m_new)
            a_blk = jnp.exp(mb_b[sl] - m_new)
            acc[rows, :] = acc[rows, :] * a_old + ob[sl] * a_blk
            l_s[rows, :] = l_s[rows, :] * a_old + lb_b[sl] * a_blk
            m_s[rows, :] = m_new

    for pi, (window, d) in enumerate(reversed(DILATION_PATTERNS)):
        assert window // d == steps
        L = S // d
        nb = L // steps
        if d == 1:
            kd[steps:, :] = k_ref[0]
            vd[steps:, :] = v_ref[0]
        elif d == d_mid:
            for src, dst, off in ((q4, qd, 0), (k4, kd, steps), (v4, vd, steps)):
                dst[off:off + S, :] = src[...].astype(BF16)
        else:
            for r in range(d):
                start = (r % d_mid) * L_mid + r // d_mid
                for src, dst, off in ((q4, qd, 0), (k4, kd, steps), (v4, vd, steps)):
                    dst[off + r * L:off + (r + 1) * L, :] = src[pl.ds(start, L, stride=d_mid), :].astype(BF16)

        def body(g, carry, pi=pi, d=d, nb=nb):
            n = g % nb
            r = g // nb
            row0 = pl.multiple_of(g * steps, steps)
            if d == 1:
                run0 = pl.multiple_of(g * run, run)
                qb = jnp.concatenate([q4[pl.ds(c * L_mid + run0, run), :] for c in range(d_mid)],
                                     axis=0).astype(BF16)
            else:
                qb = qd[pl.ds(row0, steps), :]
            kc = kd[pl.ds(row0, 2 * steps), :]
            vc = vd[pl.ds(row0, 2 * steps), :]
            s = lax.dot_general(qb, kc, (((1,), (1,)), ((), ())), preferred_element_type=F32)
            first_key = jnp.where(n > 0, 0, steps)
            in_band = band_grouped if d == 1 else band
            s = jnp.where(jnp.logical_and(in_band, km >= first_key), s, NEG_INF)
            mb = jnp.max(s, axis=-1, keepdims=True)
            p = jnp.exp(s - mb)
            lb = jnp.sum(p, axis=-1, keepdims=True)
            ob = jnp.dot(p.astype(BF16), vc, preferred_element_type=F32)
            mb_b = jnp.broadcast_to(mb, (steps, HEAD_DIM))
            lb_b = jnp.broadcast_to(lb, (steps, HEAD_DIM))
            everything = slice(0, steps)
            if d == 1:
                for c in range(d_mid):
                    merge(pl.ds(c * L_mid + run0, run), slice(c * run, (c + 1) * run),
                          ob, mb_b, lb_b, pi == 0)
            elif d == d_mid:
                merge(pl.ds(row0, steps), everything, ob, mb_b, lb_b, pi == 0)
            else:
                start = (r % d_mid) * L_mid + r // d_mid + n * (steps * d_mid)
                merge(pl.ds(start, steps, stride=d_mid), everything, ob, mb_b, lb_b, pi == 0)
            return carry

        lax.fori_loop(0, n_blocks, body, 0, unroll=ATTN_UNROLL)

    for c in range(d_mid):
        grp = slice(c * L_mid, (c + 1) * L_mid)
        o_ref[0, pl.ds(c, L_mid, stride=d_mid), :] = acc[grp, :] / l_s[grp, :]


def _attention(qkv):
    B, S, _ = qkv.shape
    H = N_HEADS_ATTN
    blk = lambda off: pl.BlockSpec((1, S, HEAD_DIM), lambda b, h: (b, 0, off + h))
    return pl.pallas_call(
        _attn_kernel,
        out_shape=jax.ShapeDtypeStruct((B, S, D_ATTN), F32),
        grid=(B, H),
        in_specs=[blk(0), blk(H), blk(2 * H)],
        out_specs=pl.BlockSpec((1, S, HEAD_DIM), lambda b, h: (b, 0, h)),
        scratch_shapes=[pltpu.VMEM((S, HEAD_DIM), F32),
                        pltpu.VMEM((S, HEAD_DIM), F32),
                        pltpu.VMEM((S, HEAD_DIM), F32),
                        pltpu.VMEM((S, HEAD_DIM), F32),
                        pltpu.VMEM((S, HEAD_DIM), BF16),
                        pltpu.VMEM((S + ATTN_STEPS, HEAD_DIM), BF16),
                        pltpu.VMEM((S + ATTN_STEPS, HEAD_DIM), BF16),
                        pltpu.VMEM((S, HEAD_DIM), F32),
                        pltpu.VMEM((S, HEAD_DIM), F32),
                        pltpu.VMEM((S, HEAD_DIM), F32)],
        compiler_params=_params(("arbitrary", "arbitrary")),
        name="attn",
    )(qkv, qkv, qkv)


SGU_TS = 512


def _sgu_kernel(u_ref, v_ref, w_ref, b_ref, lg_ref, lb_ref, o_ref):
    C = SGU_CHUNK
    n_chunks = u_ref.shape[1] // C
    row = lax.broadcasted_iota(jnp.int32, (C, C), 0)
    col = lax.broadcasted_iota(jnp.int32, (C, C), 1)
    for h in range(N_HEADS_SGU):
        cols = slice(h * HEAD_DIM, (h + 1) * HEAD_DIM)
        v = v_ref[0, :, cols]
        mu = jnp.mean(v, axis=-1, keepdims=True)
        vc = v - mu
        var = jnp.mean(vc * vc, axis=-1, keepdims=True)
        vn = vc * lax.rsqrt(var + LN_EPS) * lg_ref[h:h + 1, :] + lb_ref[h:h + 1, :]
        vn = vn.astype(BF16)
        ws = jnp.where(row >= col, w_ref[h], 0.0).astype(BF16)
        rhs = jnp.concatenate([vn[n * C:(n + 1) * C, :] for n in range(n_chunks)], axis=1)
        mixed = jnp.dot(ws, rhs, preferred_element_type=F32)
        for n in range(n_chunks):
            gate = mixed[:, n * HEAD_DIM:(n + 1) * HEAD_DIM] + b_ref[h]
            o_ref[0, n * C:(n + 1) * C, cols] = u_ref[0, n * C:(n + 1) * C, cols] * gate


def _sgu(uv, sgu_w, sgu_b_lanes, ln_g, ln_b):
    B, S, _ = uv.shape
    ts = SGU_TS
    Hs, C = N_HEADS_SGU, SGU_CHUNK
    return pl.pallas_call(
        _sgu_kernel,
        out_shape=jax.ShapeDtypeStruct((B, S, D_SGU), F32),
        grid=(B, S // ts),
        in_specs=[pl.BlockSpec((1, ts, D_SGU), lambda b, i: (b, i, 0)),
                  pl.BlockSpec((1, ts, D_SGU), lambda b, i: (b, i, 1)),
                  pl.BlockSpec((Hs, C, C), lambda b, i: (0, 0, 0)),
                  pl.BlockSpec((Hs, C, HEAD_DIM), lambda b, i: (0, 0, 0)),
                  pl.BlockSpec((Hs, HEAD_DIM), lambda b, i: (0, 0)),
                  pl.BlockSpec((Hs, HEAD_DIM), lambda b, i: (0, 0))],
        out_specs=pl.BlockSpec((1, ts, D_SGU), lambda b, i: (b, i, 0)),
        compiler_params=_params(("arbitrary", "arbitrary")),
        name="sgu",
    )(uv, uv, sgu_w, sgu_b_lanes, ln_g, ln_b)


OUTPROJ_TM = 512


def _outproj_kernel(a_ref, s_ref, x_ref, w_ref, ga_ref, gs_ref, gt_ref, sc_ref, sh_ref, g2_ref,
                    h_ref, hn_ref, hnT_ref):
    ra = _rms(a_ref[0], RMS_EPS) * ga_ref[...]
    rs = _rms(s_ref[0], RMS_EPS) * gs_ref[...]
    mixed = jnp.concatenate([ra, rs], axis=1).astype(BF16)
    y = jnp.dot(mixed, w_ref[...], preferred_element_type=F32)
    h = x_ref[0] + gt_ref[0] * y
    h_ref[0] = h
    hn = _rms(h, RMS_EPS) * g2_ref[...]
    hn = hn * (1.0 + sc_ref[0]) + sh_ref[0]
    hn_ref[0] = hn.astype(BF16)
    hnT_ref[...] = hn.T.astype(BF16)


def _outproj(attn, sgu, x, w_out, g_attn, g_sgu, gt1, sc2, sh2, g2):
    B, S, D = x.shape
    tm = OUTPROJ_TM
    nt = S // tm
    mod = lambda: pl.BlockSpec((1, 1, D), lambda b, i: (b, 0, 0))
    return pl.pallas_call(
        _outproj_kernel,
        out_shape=(jax.ShapeDtypeStruct((B, S, D), F32),
                   jax.ShapeDtypeStruct((B, S, D), BF16),
                   jax.ShapeDtypeStruct((D, B * S), BF16)),
        grid=(B, nt),
        in_specs=[pl.BlockSpec((1, tm, D_ATTN), lambda b, i: (b, i, 0)),
                  pl.BlockSpec((1, tm, D_SGU), lambda b, i: (b, i, 0)),
                  pl.BlockSpec((1, tm, D), lambda b, i: (b, i, 0)),
                  pl.BlockSpec((D, D), lambda b, i: (0, 0)),
                  pl.BlockSpec((1, D_ATTN), lambda b, i: (0, 0)),
                  pl.BlockSpec((1, D_SGU), lambda b, i: (0, 0)),
                  mod(), mod(), mod(),
                  pl.BlockSpec((1, D), lambda b, i: (0, 0))],
        out_specs=(pl.BlockSpec((1, tm, D), lambda b, i: (b, i, 0)),
                   pl.BlockSpec((1, tm, D), lambda b, i: (b, i, 0)),
                   pl.BlockSpec((D, tm), lambda b, i: (0, b * nt + i))),
        compiler_params=_params(("arbitrary", "arbitrary")),
        name="outproj",
    )(attn, sgu, x, w_out, g_attn, g_sgu, gt1, sc2, sh2, g2)


PEERK_TM = 256
TOP_ROWS = 24
N_TOP = PEER_TOPK + 1
NO_RANK = 64.0


def _extract_top(s, count, with_rank=False):
    tops = []
    rank = jnp.full(s.shape, NO_RANK, F32) if with_rank else None
    for k in range(count):
        m = jnp.max(s, axis=0, keepdims=True)
        tops.append(m)
        hit = s == m
        if with_rank:
            rank = jnp.where(hit, float(k + 1), rank)
        s = jnp.where(hit, NEG_INF, s)
    return (tops, rank) if with_rank else tops


def _count_above(b, thr):
    rows = [b[k:k + 1] for k in range(PEER_TOPK)]
    bits = []
    span = PEER_TOPK // 2
    while span >= 1:
        level = [rows[p * 2 * span + span - 1] for p in range(2 ** len(bits))]
        for m in reversed(bits):
            level = [jnp.where(m, level[2 * i + 1], level[2 * i]) for i in range(len(level) // 2)]
        bits.append(level[0] > thr)
        span //= 2
    cnt = jnp.where(rows[PEER_TOPK - 1] > thr, 1.0, 0.0)
    for i, m in enumerate(bits):
        cnt = cnt + jnp.where(m, float(PEER_TOPK >> (i + 1)), 0.0)
    return cnt


def _peerk_kernel(hn_ref, wq_ref, sk_ref, cnt_ref, e1_ref, rk2_ref, e2_ref, q_s, s1_s, top_s):
    tm = hn_ref.shape[0]
    q_s[...] = jnp.dot(hn_ref[...], wq_ref[...], preferred_element_type=F32).astype(BF16)
    pad = jnp.full((TOP_ROWS - N_TOP, tm), NEG_INF, F32)

    def scores(hp):
        col0 = pl.multiple_of(hp * PEER_NKEYS, PEER_NKEYS)
        qh = q_s[:, pl.ds(col0, PEER_NKEYS)]
        return lax.dot_general(sk_ref[hp], qh, (((1,), (1,)), ((), ())), preferred_element_type=F32)

    def score_body(h, carry):
        s1 = scores(2 * h)
        s1_s[h] = s1
        top_s[2 * h] = jnp.concatenate(_extract_top(s1, N_TOP) + [pad], axis=0)
        s2 = scores(2 * h + 1)
        tops2, rank2 = _extract_top(s2, N_TOP, with_rank=True)
        top_s[2 * h + 1] = jnp.concatenate(tops2 + [pad], axis=0)
        rk2_ref[h] = rank2.astype(BF16)
        e2_ref[h] = jnp.exp(s2 - tops2[0]).astype(BF16)
        return carry

    lax.fori_loop(0, PEER_HEADS, score_body, 0, unroll=4)

    def head_body(h, carry):
        a = top_s[2 * h]
        b = top_s[2 * h + 1]
        groups = [a[0:1] + b]
        groups += [a[i:i + 1] + b[0:SUBLANES] for i in range(1, SUBLANES)]
        groups += [a[SUBLANES:TOP_ROWS] + b[0:1]]
        best = _extract_top(jnp.concatenate(groups, axis=0), N_TOP)
        z = jnp.zeros_like(best[0])
        for v in best[:PEER_TOPK]:
            z = z + jnp.exp(v - best[0])
        tau = 0.5 * (best[PEER_TOPK - 1] + best[PEER_TOPK])
        s1 = s1_s[h]
        thr = tau - s1
        cnt_ref[h] = _count_above(b, thr)
        e1_ref[h] = jnp.exp(s1 - a[0:1]) * (0.5 / z)
        return carry

    lax.fori_loop(0, PEER_HEADS, head_body, 0, unroll=4)


def _peer_keys(hn2, w_q, sub_keys):
    T, D = hn2.shape
    tm = PEERK_TM
    H, K = PEER_HEADS, PEER_NKEYS
    out = jax.ShapeDtypeStruct((H, K, T), F32)
    out16 = jax.ShapeDtypeStruct((H, K, T), BF16)
    ospec = lambda: pl.BlockSpec((H, K, tm), lambda i: (0, 0, i))
    return pl.pallas_call(
        _peerk_kernel,
        out_shape=(out, out, out16, out16),
        grid=(T // tm,),
        in_specs=[pl.BlockSpec((tm, D), lambda i: (i, 0)),
                  pl.BlockSpec((D, 2 * H * K), lambda i: (0, 0)),
                  pl.BlockSpec((2 * H, K, K), lambda i: (0, 0, 0))],
        out_specs=(ospec(), ospec(), ospec(), ospec()),
        scratch_shapes=[pltpu.VMEM((tm, 2 * H * K), BF16),
                        pltpu.VMEM((H, K, tm), F32),
                        pltpu.VMEM((2 * H, TOP_ROWS, tm), F32)],
        compiler_params=_params(("arbitrary",)),
        name="peer_keys",
    )(hn2, w_q, sub_keys)


PEER_TT = 512
PEER_TE = 1024


def _peer_kernel(hT_ref, u_ref, v_ref, cnt_ref, e1_ref, rk2_ref, e2_ref, h1_ref, gt_ref, gf_ref,
                 out_ref, p0, p1, a0, a1, o_s, *, chunks_per_tile):
    g = pl.program_id(0)

    @pl.when(g == 0)
    def _():
        p1[...] = jnp.zeros(p1.shape, p1.dtype)

    @pl.when(jnp.logical_or(g == 0, (g + chunks_per_tile - 1) % chunks_per_tile == 0))
    def _():
        o_s[...] = jnp.zeros(o_s.shape, o_s.dtype)

    K = PEER_NKEYS
    n_c = PEER_TE // K
    d_cols = o_s.shape[1] // n_c

    def pre_act(c, a_w):
        rows = pl.ds(pl.multiple_of(c * K, K), K)
        a_w[...] = jnp.dot(u_ref[rows, :], hT_ref[...], preferred_element_type=F32)

    def piece(c, a_r, a_w, p_w, p_r, with_next):
        zero = jnp.zeros((), BF16)
        gate = None
        for h in range(PEER_HEADS):
            cnt = cnt_ref[h, pl.ds(c, 1), :].astype(BF16)
            picked = jnp.where(rk2_ref[h] <= cnt, e2_ref[h], zero)
            term = e1_ref[h, pl.ds(c, 1), :].astype(BF16) * picked
            gate = term if gate is None else gate + term
        p_new = (gate * _gelu_doubled(a_r[...]).astype(BF16)).T
        dc = pl.ds(pl.multiple_of(c * d_cols, d_cols), d_cols)
        o_new = o_s[:, dc] + jnp.dot(p_r[...], v_ref[:, dc], preferred_element_type=F32)
        if with_next:
            rows = pl.ds(pl.multiple_of((c + 1) * K, K), K)
            a_new = jnp.dot(u_ref[rows, :], hT_ref[...], preferred_element_type=F32)
        p_w[:, pl.ds(pl.multiple_of(c * K, K), K)] = p_new
        o_s[:, dc] = o_new
        if with_next:
            a_w[...] = a_new

    def step(p_w, p_r):
        pre_act(0, a0)

        def body(i, carry):
            piece(2 * i, a0, a1, p_w, p_r, True)
            piece(2 * i + 1, a1, a0, p_w, p_r, True)
            return carry

        lax.fori_loop(0, n_c // 2 - 1, body, 0)
        piece(n_c - 2, a0, a1, p_w, p_r, True)
        piece(n_c - 1, a1, a0, p_w, p_r, False)

    @pl.when(g % 2 == 0)
    def _():
        step(p0, p1)

    @pl.when(g % 2 == 1)
    def _():
        step(p1, p0)

    @pl.when(jnp.logical_and(g > 0, g % chunks_per_tile == 0))
    def _():
        h = h1_ref[...] + gt_ref[0] * o_s[...]
        out_ref[...] = _rms(h, RMS_EPS) * gf_ref[...]


def _peer_ffn(hn2T, u, v, cnt, e1, rk2, e2, h1, gt2, g_final):
    D, T = hn2T.shape
    E = u.shape[0]
    tt, te = PEER_TT, PEER_TE
    H, K = PEER_HEADS, PEER_NKEYS
    nj = E // te
    n_chunks = (T // tt) * nj
    tiles_per_batch = T // gt2.shape[0] // tt
    act = lambda g: jnp.minimum(g, n_chunks - 1)
    val_tile = lambda g: jnp.maximum(g - 1, 0) // nj
    per_c = pl.BlockSpec((H, te // K, tt), lambda g: (0, act(g) % nj, act(g) // nj))
    per_t = pl.BlockSpec((H, K, tt), lambda g: (0, 0, act(g) // nj))
    return pl.pallas_call(
        functools.partial(_peer_kernel, chunks_per_tile=nj),
        out_shape=jax.ShapeDtypeStruct((T, D), F32),
        grid=(n_chunks + 1,),
        in_specs=[pl.BlockSpec((D, tt), lambda g: (0, act(g) // nj)),
                  pl.BlockSpec((te, D), lambda g: (act(g) % nj, 0)),
                  pl.BlockSpec((te, D), lambda g: (jnp.maximum(g - 1, 0) % nj, 0)),
                  per_c, per_c, per_t, per_t,
                  pl.BlockSpec((tt, D), lambda g: (val_tile(g), 0)),
                  pl.BlockSpec((1, 1, D), lambda g: (val_tile(g) // tiles_per_batch, 0, 0)),
                  pl.BlockSpec((1, D), lambda g: (0, 0))],
        out_specs=pl.BlockSpec((tt, D), lambda g: (val_tile(g), 0)),
        scratch_shapes=[pltpu.VMEM((tt, te), BF16),
                        pltpu.VMEM((tt, te), BF16),
                        pltpu.VMEM((K, tt), F32),
                        pltpu.VMEM((K, tt), F32),
                        pltpu.VMEM((tt, D), F32)],
        compiler_params=_params(("arbitrary",)),
        name="peer_ffn",
    )(hn2T, u, v, cnt, e1, rk2, e2, h1, gt2, g_final)


def kernel(x, c, positions, w_ada, b_ada, g_norm1, w_in, g_attn_out, g_sgu_out, sgu_w, sgu_b,
           sgu_ln_g, sgu_ln_b, w_out, g_norm2, peer_w_q, peer_sub_keys, peer_u, peer_v, g_final):
    B, S, D = x.shape
    assert w_ada.shape[0] == 1, "single-layer block"
    l = 0
    half = HEAD_DIM // 2
    inv = ROPE_THETA ** (-jnp.arange(half, dtype=F32) / half)
    inv2 = jnp.concatenate([inv, inv]).reshape(1, HEAD_DIM)
    posf = positions.astype(F32).reshape(B, S, 1)
    row = lambda g: g.reshape(1, -1)

    mod = _adaln(c, w_ada[l], b_ada[l]).reshape(B, N_MOD, 1, D)
    sh1, sc1, gt1, sh2, sc2, gt2 = [mod[:, i] for i in range(N_MOD)]

    qkv, uv = _inproj(x, posf, inv2, sc1, sh1, row(g_norm1[l]), w_in[l].astype(BF16))
    attn = _attention(qkv)
    sgu_b_lanes = jnp.broadcast_to(sgu_b[l][:, :, None], (N_HEADS_SGU, SGU_CHUNK, HEAD_DIM))
    sgu = _sgu(uv, sgu_w[l], sgu_b_lanes, sgu_ln_g[l], sgu_ln_b[l])
    h1, hn2, hn2T = _outproj(attn, sgu, x, w_out[l].astype(BF16), row(g_attn_out[l]),
                             row(g_sgu_out[l]), gt1, sc2, sh2, row(g_norm2[l]))

    sk = peer_sub_keys[l].astype(BF16).reshape(2 * PEER_HEADS, PEER_NKEYS, -1)
    cnt, e1, rk2, e2 = _peer_keys(hn2.reshape(B * S, D), peer_w_q[l].astype(BF16), sk)
    out = _peer_ffn(hn2T, peer_u[l].astype(BF16), peer_v[l].astype(BF16), cnt, e1, rk2, e2,
                    h1.reshape(B * S, D), gt2, row(g_final))
    return out.reshape(B, S, D)
```

```python
import functools
import math

import jax
import jax.numpy as jnp
from jax import lax
from jax.experimental import pallas as pl
from jax.experimental.pallas import tpu as pltpu

F32 = jnp.float32
BF16 = jnp.bfloat16

HEAD_DIM = 128
N_HEADS_ATTN = 12
N_HEADS_SGU = 4
D_ATTN = N_HEADS_ATTN * HEAD_DIM
D_SGU = N_HEADS_SGU * HEAD_DIM
D_QKV = 3 * D_ATTN
DILATION_PATTERNS = ((128, 1), (512, 4), (2048, 16))
ATTN_STEPS = 128
ATTN_UNROLL = 32
ROPE_THETA = 10000.0
SGU_CHUNK = 128
PEER_HEADS = 8
PEER_NKEYS = 128
PEER_TOPK = 16
N_MOD = 6
RMS_EPS = 1e-6
LN_EPS = 1e-5

LANES = 128
SUBLANES = 8
VMEM_LIMIT = 56 * 1024 * 1024
NEG_INF = float("-inf")


def _params(semantics):
    return pltpu.CompilerParams(dimension_semantics=semantics, vmem_limit_bytes=VMEM_LIMIT)


def _gelu(x):
    return 0.5 * x * (1.0 + lax.erf(x * (1.0 / math.sqrt(2.0))))


def _gelu_doubled(x):
    return x * (1.0 + lax.erf(x * (1.0 / math.sqrt(2.0))))


def _rms(x, eps):
    return x * lax.rsqrt(jnp.mean(x * x, axis=-1, keepdims=True) + eps)


def _adaln_kernel(c_ref, w_ref, b_ref, o_ref):
    c = c_ref[...]
    cond = c * jax.nn.sigmoid(c)
    o_ref[...] = jnp.dot(cond.astype(BF16), w_ref[...].astype(BF16),
                         preferred_element_type=F32) + b_ref[...]


def _adaln(c, w_ada, b_ada):
    B, D = c.shape
    N = w_ada.shape[1]
    tn = 1024
    rows = SUBLANES
    c_pad = jnp.pad(c, ((0, rows - B), (0, 0)))
    out = pl.pallas_call(
        _adaln_kernel,
        out_shape=jax.ShapeDtypeStruct((rows, N), F32),
        grid=(N // tn,),
        in_specs=[pl.BlockSpec((rows, D), lambda j: (0, 0)),
                  pl.BlockSpec((D, tn), lambda j: (0, j)),
                  pl.BlockSpec((1, tn), lambda j: (0, j))],
        out_specs=pl.BlockSpec((rows, tn), lambda j: (0, j)),
        compiler_params=_params(("arbitrary",)),
        name="adaln",
    )(c_pad, w_ada, b_ada.reshape(1, N))
    return out[:B]


INPROJ_TM = 1024
INPROJ_TN = 512
_QK_TILES = 2 * D_ATTN // INPROJ_TN
_Q_TILES = D_ATTN // INPROJ_TN
_QKV_TILES = D_QKV // INPROJ_TN


def _inproj_kernel(x_ref, pos_ref, inv_ref, sc_ref, sh_ref, g_ref, w_ref,
                   qkv_ref, uv_ref, hn_ref, cos_ref, sin_ref):
    j = pl.program_id(2)

    @pl.when(j == 0)
    def _():
        x = x_ref[0]
        hn = _rms(x, RMS_EPS) * g_ref[...]
        hn = hn * (1.0 + sc_ref[0]) + sh_ref[0]
        hn_ref[...] = hn.astype(BF16)
        ang = pos_ref[0] * inv_ref[...]
        lane = lax.broadcasted_iota(jnp.int32, ang.shape, 1)
        cos_ref[...] = jnp.cos(ang)
        sin_ref[...] = jnp.where(lane < HEAD_DIM // 2, -1.0, 1.0) * jnp.sin(ang)

    def project():
        return jnp.dot(hn_ref[...], w_ref[...], preferred_element_type=F32)

    @pl.when(j < _QK_TILES)
    def _():
        scale = jnp.where(j < _Q_TILES, HEAD_DIM ** -0.5, 1.0).astype(F32)
        cos = cos_ref[...] * scale
        sin = sin_ref[...] * scale
        acc = project()
        for h in range(INPROJ_TN // HEAD_DIM):
            cols = slice(h * HEAD_DIM, (h + 1) * HEAD_DIM)
            t = acc[:, cols]
            r = t * cos + pltpu.roll(t, HEAD_DIM // 2, 1) * sin
            qkv_ref[0, :, cols] = r.astype(BF16)

    @pl.when(jnp.logical_and(j >= _QK_TILES, j < _QKV_TILES))
    def _():
        qkv_ref[0] = project().astype(BF16)

    @pl.when(j >= _QKV_TILES)
    def _():
        uv_ref[0] = _gelu(project())


def _inproj(x, posf, inv2, sc1, sh1, g1, w_in):
    B, S, D = x.shape
    N = w_in.shape[1]
    tm, tn = INPROJ_TM, INPROJ_TN
    last_qkv = _QKV_TILES - 1
    return pl.pallas_call(
        _inproj_kernel,
        out_shape=(jax.ShapeDtypeStruct((B, S, D_QKV), BF16),
                   jax.ShapeDtypeStruct((B, S, 2 * D_SGU), F32)),
        grid=(B, S // tm, N // tn),
        in_specs=[pl.BlockSpec((1, tm, D), lambda b, i, j: (b, i, 0)),
                  pl.BlockSpec((1, tm, 1), lambda b, i, j: (b, i, 0)),
                  pl.BlockSpec((1, LANES), lambda b, i, j: (0, 0)),
                  pl.BlockSpec((1, 1, D), lambda b, i, j: (b, 0, 0)),
                  pl.BlockSpec((1, 1, D), lambda b, i, j: (b, 0, 0)),
                  pl.BlockSpec((1, D), lambda b, i, j: (0, 0)),
                  pl.BlockSpec((D, tn), lambda b, i, j: (0, j))],
        out_specs=(pl.BlockSpec((1, tm, tn), lambda b, i, j: (b, i, jnp.minimum(j, last_qkv))),
                   pl.BlockSpec((1, tm, tn), lambda b, i, j: (b, i, jnp.maximum(j - _QKV_TILES, 0)))),
        scratch_shapes=[pltpu.VMEM((tm, D), BF16),
                        pltpu.VMEM((tm, LANES), F32),
                        pltpu.VMEM((tm, LANES), F32)],
        compiler_params=_params(("arbitrary", "arbitrary", "arbitrary")),
        name="inproj",
    )(x, posf, inv2, sc1, sh1, g1, w_in)


def _attn_kernel(q_ref, k_ref, v_ref, o_ref, nat, q4, k4, v4, qd, kd, vd, acc, m_s, l_s):
    S = q_ref.shape[1]
    steps = ATTN_STEPS
    n_blocks = S // steps
    (_, d_min), (_, d_mid), (_, d_max) = DILATION_PATTERNS
    assert d_min == 1 and d_max == d_mid * d_mid
    L_mid = S // d_mid
    for src_ref, grouped in ((q_ref, q4), (k_ref, k4), (v_ref, v4)):
        nat[...] = src_ref[0].astype(F32)
        for r in range(d_mid):
            grouped[r * L_mid:(r + 1) * L_mid, :] = nat[pl.ds(r, L_mid, stride=d_mid), :]
    kd[0:steps, :] = jnp.zeros((steps, HEAD_DIM), BF16)
    vd[0:steps, :] = jnp.zeros((steps, HEAD_DIM), BF16)

    row = lax.broadcasted_iota(jnp.int32, (steps, 2 * steps), 0)
    km = lax.broadcasted_iota(jnp.int32, (steps, 2 * steps), 1)
    band = jnp.logical_and(km >= row, km <= row + steps)
    run = steps // d_mid
    perm = (row % run) * d_mid + row // run
    band_grouped = jnp.logical_and(km >= perm, km <= perm + steps)

    def merge(rows, sl, ob, mb_b, lb_b, init):
        if init:
            acc[rows, :] = ob[sl]
            m_s[rows, :] = mb_b[sl]
            l_s[rows, :] = lb_b[sl]
        else:
            m_old = m_s[rows, :]
            m_new = jnp.maximum(m_old, mb_b[sl])
            a_old = jnp.exp(m_old - m_new)
            a_blk = jnp.exp(mb_b[sl] - m_new)
            acc[rows, :] = acc[rows, :] * a_old + ob[sl] * a_blk
            l_s[rows, :] = l_s[rows, :] * a_old + lb_b[sl] * a_blk
            m_s[rows, :] = m_new

    for pi, (window, d) in enumerate(reversed(DILATION_PATTERNS)):
        assert window // d == steps
        L = S // d
        nb = L // steps
        if d == 1:
            kd[steps:, :] = k_ref[0]
            vd[steps:, :] = v_ref[0]
        elif d == d_mid:
            for src, dst, off in ((q4, qd, 0), (k4, kd, steps), (v4, vd, steps)):
                dst[off:off + S, :] = src[...].astype(BF16)
        else:
            for r in range(d):
                start = (r % d_mid) * L_mid + r // d_mid
                for src, dst, off in ((q4, qd, 0), (k4, kd, steps), (v4, vd, steps)):
                    dst[off + r * L:off + (r + 1) * L, :] = src[pl.ds(start, L, stride=d_mid), :].astype(BF16)

        def body(g, carry, pi=pi, d=d, nb=nb):
            n = g % nb
            r = g // nb
            row0 = pl.multiple_of(g * steps, steps)
            if d == 1:
                run0 = pl.multiple_of(g * run, run)
                qb = jnp.concatenate([q4[pl.ds(c * L_mid + run0, run), :] for c in range(d_mid)],
                                     axis=0).astype(BF16)
            else:
                qb = qd[pl.ds(row0, steps), :]
            kc = kd[pl.ds(row0, 2 * steps), :]
            vc = vd[pl.ds(row0, 2 * steps), :]
            s = lax.dot_general(qb, kc, (((1,), (1,)), ((), ())), preferred_element_type=F32)
            first_key = jnp.where(n > 0, 0, steps)
            in_band = band_grouped if d == 1 else band
            s = jnp.where(jnp.logical_and(in_band, km >= first_key), s, NEG_INF)
            mb = jnp.max(s, axis=-1, keepdims=True)
            p = jnp.exp(s - mb)
            lb = jnp.sum(p, axis=-1, keepdims=True)
            ob = jnp.dot(p.astype(BF16), vc, preferred_element_type=F32)
            mb_b = jnp.broadcast_to(mb, (steps, HEAD_DIM))
            lb_b = jnp.broadcast_to(lb, (steps, HEAD_DIM))
            everything = slice(0, steps)
            if d == 1:
                for c in range(d_mid):
                    merge(pl.ds(c * L_mid + run0, run), slice(c * run, (c + 1) * run),
                          ob, mb_b, lb_b, pi == 0)
            elif d == d_mid:
                merge(pl.ds(row0, steps), everything, ob, mb_b, lb_b, pi == 0)
            else:
                start = (r % d_mid) * L_mid + r // d_mid + n * (steps * d_mid)
                merge(pl.ds(start, steps, stride=d_mid), everything, ob, mb_b, lb_b, pi == 0)
            return carry

        lax.fori_loop(0, n_blocks, body, 0, unroll=ATTN_UNROLL)

    for c in range(d_mid):
        grp = slice(c * L_mid, (c + 1) * L_mid)
        o_ref[0, pl.ds(c, L_mid, stride=d_mid), :] = acc[grp, :] / l_s[grp, :]


def _attention(qkv):
    B, S, _ = qkv.shape
    H = N_HEADS_ATTN
    blk = lambda off: pl.BlockSpec((1, S, HEAD_DIM), lambda b, h: (b, 0, off + h))
    return pl.pallas_call(
        _attn_kernel,
        out_shape=jax.ShapeDtypeStruct((B, S, D_ATTN), F32),
        grid=(B, H),
        in_specs=[blk(0), blk(H), blk(2 * H)],
        out_specs=pl.BlockSpec((1, S, HEAD_DIM), lambda b, h: (b, 0, h)),
        scratch_shapes=[pltpu.VMEM((S, HEAD_DIM), F32),
                        pltpu.VMEM((S, HEAD_DIM), F32),
                        pltpu.VMEM((S, HEAD_DIM), F32),
                        pltpu.VMEM((S, HEAD_DIM), F32),
                        pltpu.VMEM((S, HEAD_DIM), BF16),
                        pltpu.VMEM((S + ATTN_STEPS, HEAD_DIM), BF16),
                        pltpu.VMEM((S + ATTN_STEPS, HEAD_DIM), BF16),
                        pltpu.VMEM((S, HEAD_DIM), F32),
                        pltpu.VMEM((S, HEAD_DIM), F32),
                        pltpu.VMEM((S, HEAD_DIM), F32)],
        compiler_params=_params(("arbitrary", "arbitrary")),
        name="attn",
    )(qkv, qkv, qkv)


SGU_TS = 512


def _sgu_kernel(u_ref, v_ref, w_ref, b_ref, lg_ref, lb_ref, o_ref):
    C = SGU_CHUNK
    n_chunks = u_ref.shape[1] // C
    row = lax.broadcasted_iota(jnp.int32, (C, C), 0)
    col = lax.broadcasted_iota(jnp.int32, (C, C), 1)
    for h in range(N_HEADS_SGU):
        cols = slice(h * HEAD_DIM, (h + 1) * HEAD_DIM)
        v = v_ref[0, :, cols]
        mu = jnp.mean(v, axis=-1, keepdims=True)
        vc = v - mu
        var = jnp.mean(vc * vc, axis=-1, keepdims=True)
        vn = vc * lax.rsqrt(var + LN_EPS) * lg_ref[h:h + 1, :] + lb_ref[h:h + 1, :]
        vn = vn.astype(BF16)
        ws = jnp.where(row >= col, w_ref[h], 0.0).astype(BF16)
        rhs = jnp.concatenate([vn[n * C:(n + 1) * C, :] for n in range(n_chunks)], axis=1)
        mixed = jnp.dot(ws, rhs, preferred_element_type=F32)
        for n in range(n_chunks):
            gate = mixed[:, n * HEAD_DIM:(n + 1) * HEAD_DIM] + b_ref[h]
            o_ref[0, n * C:(n + 1) * C, cols] = u_ref[0, n * C:(n + 1) * C, cols] * gate


def _sgu(uv, sgu_w, sgu_b_lanes, ln_g, ln_b):
    B, S, _ = uv.shape
    ts = SGU_TS
    Hs, C = N_HEADS_SGU, SGU_CHUNK
    return pl.pallas_call(
        _sgu_kernel,
        out_shape=jax.ShapeDtypeStruct((B, S, D_SGU), F32),
        grid=(B, S // ts),
        in_specs=[pl.BlockSpec((1, ts, D_SGU), lambda b, i: (b, i, 0)),
                  pl.BlockSpec((1, ts, D_SGU), lambda b, i: (b, i, 1)),
                  pl.BlockSpec((Hs, C, C), lambda b, i: (0, 0, 0)),
                  pl.BlockSpec((Hs, C, HEAD_DIM), lambda b, i: (0, 0, 0)),
                  pl.BlockSpec((Hs, HEAD_DIM), lambda b, i: (0, 0)),
                  pl.BlockSpec((Hs, HEAD_DIM), lambda b, i: (0, 0))],
        out_specs=pl.BlockSpec((1, ts, D_SGU), lambda b, i: (b, i, 0)),
        compiler_params=_params(("arbitrary", "arbitrary")),
        name="sgu",
    )(uv, uv, sgu_w, sgu_b_lanes, ln_g, ln_b)


OUTPROJ_TM = 512


def _outproj_kernel(a_ref, s_ref, x_ref, w_ref, ga_ref, gs_ref, gt_ref, sc_ref, sh_ref, g2_ref,
                    h_ref, hn_ref, hnT_ref):
    ra = _rms(a_ref[0], RMS_EPS) * ga_ref[...]
    rs = _rms(s_ref[0], RMS_EPS) * gs_ref[...]
    mixed = jnp.concatenate([ra, rs], axis=1).astype(BF16)
    y = jnp.dot(mixed, w_ref[...], preferred_element_type=F32)
    h = x_ref[0] + gt_ref[0] * y
    h_ref[0] = h
    hn = _rms(h, RMS_EPS) * g2_ref[...]
    hn = hn * (1.0 + sc_ref[0]) + sh_ref[0]
    hn_ref[0] = hn.astype(BF16)
    hnT_ref[...] = hn.T.astype(BF16)


def _outproj(attn, sgu, x, w_out, g_attn, g_sgu, gt1, sc2, sh2, g2):
    B, S, D = x.shape
    tm = OUTPROJ_TM
    nt = S // tm
    mod = lambda: pl.BlockSpec((1, 1, D), lambda b, i: (b, 0, 0))
    return pl.pallas_call(
        _outproj_kernel,
        out_shape=(jax.ShapeDtypeStruct((B, S, D), F32),
                   jax.ShapeDtypeStruct((B, S, D), BF16),
                   jax.ShapeDtypeStruct((D, B * S), BF16)),
        grid=(B, nt),
        in_specs=[pl.BlockSpec((1, tm, D_ATTN), lambda b, i: (b, i, 0)),
                  pl.BlockSpec((1, tm, D_SGU), lambda b, i: (b, i, 0)),
                  pl.BlockSpec((1, tm, D), lambda b, i: (b, i, 0)),
                  pl.BlockSpec((D, D), lambda b, i: (0, 0)),
                  pl.BlockSpec((1, D_ATTN), lambda b, i: (0, 0)),
                  pl.BlockSpec((1, D_SGU), lambda b, i: (0, 0)),
                  mod(), mod(), mod(),
                  pl.BlockSpec((1, D), lambda b, i: (0, 0))],
        out_specs=(pl.BlockSpec((1, tm, D), lambda b, i: (b, i, 0)),
                   pl.BlockSpec((1, tm, D), lambda b, i: (b, i, 0)),
                   pl.BlockSpec((D, tm), lambda b, i: (0, b * nt + i))),
        compiler_params=_params(("arbitrary", "arbitrary")),
        name="outproj",
    )(attn, sgu, x, w_out, g_attn, g_sgu, gt1, sc2, sh2, g2)


PEERK_TM = 256
TOP_ROWS = 24
N_TOP = PEER_TOPK + 1
NO_RANK = 64.0


def _extract_top(s, count, with_rank=False):
    tops = []
    rank = jnp.full(s.shape, NO_RANK, F32) if with_rank else None
    for k in range(count):
        m = jnp.max(s, axis=0, keepdims=True)
        tops.append(m)
        hit = s == m
        if with_rank:
            rank = jnp.where(hit, float(k + 1), rank)
        s = jnp.where(hit, NEG_INF, s)
    return (tops, rank) if with_rank else tops


def _count_above(b, thr):
    rows = [b[k:k + 1] for k in range(PEER_TOPK)]
    bits = []
    span = PEER_TOPK // 2
    while span >= 1:
        level = [rows[p * 2 * span + span - 1] for p in range(2 ** len(bits))]
        for m in reversed(bits):
            level = [jnp.where(m, level[2 * i + 1], level[2 * i]) for i in range(len(level) // 2)]
        bits.append(level[0] > thr)
        span //= 2
    cnt = jnp.where(rows[PEER_TOPK - 1] > thr, 1.0, 0.0)
    for i, m in enumerate(bits):
        cnt = cnt + jnp.where(m, float(PEER_TOPK >> (i + 1)), 0.0)
    return cnt


def _peerk_kernel(hn_ref, wq_ref, sk_ref, cnt_ref, e1_ref, rk2_ref, e2_ref, q_s, s1_s, top_s):
    tm = hn_ref.shape[0]
    q_s[...] = jnp.dot(hn_ref[...], wq_ref[...], preferred_element_type=F32).astype(BF16)
    pad = jnp.full((TOP_ROWS - N_TOP, tm), NEG_INF, F32)

    def scores(hp):
        col0 = pl.multiple_of(hp * PEER_NKEYS, PEER_NKEYS)
        qh = q_s[:, pl.ds(col0, PEER_NKEYS)]
        return lax.dot_general(sk_ref[hp], qh, (((1,), (1,)), ((), ())), preferred_element_type=F32)

    def score_body(h, carry):
        s1 = scores(2 * h)
        s1_s[h] = s1
        top_s[2 * h] = jnp.concatenate(_extract_top(s1, N_TOP) + [pad], axis=0)
        s2 = scores(2 * h + 1)
        tops2, rank2 = _extract_top(s2, N_TOP, with_rank=True)
        top_s[2 * h + 1] = jnp.concatenate(tops2 + [pad], axis=0)
        rk2_ref[h] = rank2.astype(BF16)
        e2_ref[h] = jnp.exp(s2 - tops2[0]).astype(BF16)
        return carry

    lax.fori_loop(0, PEER_HEADS, score_body, 0, unroll=4)

    def head_body(h, carry):
        a = top_s[2 * h]
        b = top_s[2 * h + 1]
        groups = [a[0:1] + b]
        groups += [a[i:i + 1] + b[0:SUBLANES] for i in range(1, SUBLANES)]
        groups += [a[SUBLANES:TOP_ROWS] + b[0:1]]
        best = _extract_top(jnp.concatenate(groups, axis=0), N_TOP)
        z = jnp.zeros_like(best[0])
        for v in best[:PEER_TOPK]:
            z = z + jnp.exp(v - best[0])
        tau = 0.5 * (best[PEER_TOPK - 1] + best[PEER_TOPK])
        s1 = s1_s[h]
        thr = tau - s1
        cnt_ref[h] = _count_above(b, thr)
        e1_ref[h] = jnp.exp(s1 - a[0:1]) * (0.5 / z)
        return carry

    lax.fori_loop(0, PEER_HEADS, head_body, 0, unroll=4)


def _peer_keys(hn2, w_q, sub_keys):
    T, D = hn2.shape
    tm = PEERK_TM
    H, K = PEER_HEADS, PEER_NKEYS
    out = jax.ShapeDtypeStruct((H, K, T), F32)
    out16 = jax.ShapeDtypeStruct((H, K, T), BF16)
    ospec = lambda: pl.BlockSpec((H, K, tm), lambda i: (0, 0, i))
    return pl.pallas_call(
        _peerk_kernel,
        out_shape=(out, out, out16, out16),
        grid=(T // tm,),
        in_specs=[pl.BlockSpec((tm, D), lambda i: (i, 0)),
                  pl.BlockSpec((D, 2 * H * K), lambda i: (0, 0)),
                  pl.BlockSpec((2 * H, K, K), lambda i: (0, 0, 0))],
        out_specs=(ospec(), ospec(), ospec(), ospec()),
        scratch_shapes=[pltpu.VMEM((tm, 2 * H * K), BF16),
                        pltpu.VMEM((H, K, tm), F32),
                        pltpu.VMEM((2 * H, TOP_ROWS, tm), F32)],
        compiler_params=_params(("arbitrary",)),
        name="peer_keys",
    )(hn2, w_q, sub_keys)


PEER_TT = 512
PEER_TE = 1024


def _peer_kernel(hT_ref, u_ref, vt_ref, cnt_ref, e1_ref, rk2_ref, e2_ref, h1_ref, gt_ref, gf_ref,
                 out_ref, p0, p1, a0, a1, o_s, *, chunks_per_tile):
    g = pl.program_id(0)

    @pl.when(g == 0)
    def _():
        p1[...] = jnp.zeros(p1.shape, p1.dtype)

    @pl.when(jnp.logical_or(g == 0, (g + chunks_per_tile - 1) % chunks_per_tile == 0))
    def _():
        o_s[...] = jnp.zeros(o_s.shape, o_s.dtype)

    K = PEER_NKEYS
    n_c = PEER_TE // K
    d_rows = o_s.shape[0] // n_c

    def pre_act(c, a_w):
        rows = pl.ds(pl.multiple_of(c * K, K), K)
        a_w[...] = jnp.dot(u_ref[rows, :], hT_ref[...], preferred_element_type=F32)

    def piece(c, a_r, a_w, p_w, p_r, with_next):
        zero = jnp.zeros((), BF16)
        gate = None
        for h in range(PEER_HEADS):
            cnt = cnt_ref[h, pl.ds(c, 1), :].astype(BF16)
            picked = jnp.where(rk2_ref[h] <= cnt, e2_ref[h], zero)
            term = e1_ref[h, pl.ds(c, 1), :].astype(BF16) * picked
            gate = term if gate is None else gate + term
        p_new = gate * _gelu_doubled(a_r[...].astype(BF16))
        dr = pl.ds(pl.multiple_of(c * d_rows, d_rows), d_rows)
        o_new = o_s[dr, :] + jnp.dot(vt_ref[dr, :], p_r[...], preferred_element_type=F32)
        if with_next:
            rows = pl.ds(pl.multiple_of((c + 1) * K, K), K)
            a_new = jnp.dot(u_ref[rows, :], hT_ref[...], preferred_element_type=F32)
        p_w[pl.ds(pl.multiple_of(c * K, K), K), :] = p_new
        o_s[dr, :] = o_new
        if with_next:
            a_w[...] = a_new

    def step(p_w, p_r):
        pre_act(0, a0)

        def body(i, carry):
            piece(2 * i, a0, a1, p_w, p_r, True)
            piece(2 * i + 1, a1, a0, p_w, p_r, True)
            return carry

        lax.fori_loop(0, n_c // 2 - 1, body, 0)
        piece(n_c - 2, a0, a1, p_w, p_r, True)
        piece(n_c - 1, a1, a0, p_w, p_r, False)

    @pl.when(g % 2 == 0)
    def _():
        step(p0, p1)

    @pl.when(g % 2 == 1)
    def _():
        step(p1, p0)

    @pl.when(jnp.logical_and(g > 0, g % chunks_per_tile == 0))
    def _():
        h = h1_ref[...] + gt_ref[0] * o_s[...].T
        out_ref[...] = _rms(h, RMS_EPS) * gf_ref[...]


def _peer_ffn(hn2T, u, vT, cnt, e1, rk2, e2, h1, gt2, g_final):
    D, T = hn2T.shape
    E = u.shape[0]
    tt, te = PEER_TT, PEER_TE
    H, K = PEER_HEADS, PEER_NKEYS
    nj = E // te
    n_chunks = (T // tt) * nj
    tiles_per_batch = T // gt2.shape[0] // tt
    act = lambda g: jnp.minimum(g, n_chunks - 1)
    val_tile = lambda g: jnp.maximum(g - 1, 0) // nj
    per_c = pl.BlockSpec((H, te // K, tt), lambda g: (0, act(g) % nj, act(g) // nj))
    per_t = pl.BlockSpec((H, K, tt), lambda g: (0, 0, act(g) // nj))
    return pl.pallas_call(
        functools.partial(_peer_kernel, chunks_per_tile=nj),
        out_shape=jax.ShapeDtypeStruct((T, D), F32),
        grid=(n_chunks + 1,),
        in_specs=[pl.BlockSpec((D, tt), lambda g: (0, act(g) // nj)),
                  pl.BlockSpec((te, D), lambda g: (act(g) % nj, 0)),
                  pl.BlockSpec((D, te), lambda g: (0, jnp.maximum(g - 1, 0) % nj)),
                  per_c, per_c, per_t, per_t,
                  pl.BlockSpec((tt, D), lambda g: (val_tile(g), 0)),
                  pl.BlockSpec((1, 1, D), lambda g: (val_tile(g) // tiles_per_batch, 0, 0)),
                  pl.BlockSpec((1, D), lambda g: (0, 0))],
        out_specs=pl.BlockSpec((tt, D), lambda g: (val_tile(g), 0)),
        scratch_shapes=[pltpu.VMEM((te, tt), BF16),
                        pltpu.VMEM((te, tt), BF16),
                        pltpu.VMEM((K, tt), F32),
                        pltpu.VMEM((K, tt), F32),
                        pltpu.VMEM((D, tt), F32)],
        compiler_params=_params(("arbitrary",)),
        name="peer_ffn",
    )(hn2T, u, vT, cnt, e1, rk2, e2, h1, gt2, g_final)


def kernel(x, c, positions, w_ada, b_ada, g_norm1, w_in, g_attn_out, g_sgu_out, sgu_w, sgu_b,
           sgu_ln_g, sgu_ln_b, w_out, g_norm2, peer_w_q, peer_sub_keys, peer_u, peer_v, g_final):
    B, S, D = x.shape
    assert w_ada.shape[0] == 1, "single-layer block"
    l = 0
    half = HEAD_DIM // 2
    inv = ROPE_THETA ** (-jnp.arange(half, dtype=F32) / half)
    inv2 = jnp.concatenate([inv, inv]).reshape(1, HEAD_DIM)
    posf = positions.astype(F32).reshape(B, S, 1)
    row = lambda g: g.reshape(1, -1)

    mod = _adaln(c, w_ada[l], b_ada[l]).reshape(B, N_MOD, 1, D)
    sh1, sc1, gt1, sh2, sc2, gt2 = [mod[:, i] for i in range(N_MOD)]

    qkv, uv = _inproj(x, posf, inv2, sc1, sh1, row(g_norm1[l]), w_in[l].astype(BF16))
    attn = _attention(qkv)
    sgu_b_lanes = jnp.broadcast_to(sgu_b[l][:, :, None], (N_HEADS_SGU, SGU_CHUNK, HEAD_DIM))
    sgu = _sgu(uv, sgu_w[l], sgu_b_lanes, sgu_ln_g[l], sgu_ln_b[l])
    h1, hn2, hn2T = _outproj(attn, sgu, x, w_out[l].astype(BF16), row(g_attn_out[l]),
                             row(g_sgu_out[l]), gt1, sc2, sh2, row(g_norm2[l]))

    sk = peer_sub_keys[l].astype(BF16).reshape(2 * PEER_HEADS, PEER_NKEYS, -1)
    cnt, e1, rk2, e2 = _peer_keys(hn2.reshape(B * S, D), peer_w_q[l].astype(BF16), sk)
    out = _peer_ffn(hn2T, peer_u[l].astype(BF16), peer_v[l].T.astype(BF16), cnt, e1, rk2, e2,
                    h1.reshape(B * S, D), gt2, row(g_final))
    return out.reshape(B, S, D)
```
